```python
import math
import jax
import jax.numpy as jnp
from jax import lax
import numpy as np

D_MODEL = 1024
BATCH = 4
SEQ = 4096
DEPTH = 2
DEC_BATCH = 32
DEC_SEQ = 8
PAST_LEN = 8192
PAGE_SIZE = 128

H_ATT = 8
HD_ATT = 64
W_ATT = H_ATT * HD_ATT
Q_BLOCK = 128
FORGET_BIAS_INIT = 3.0
W_CONV = D_MODEL // 2
CONV_WIDTH = 31
W_RG = D_MODEL // 2
H_RG = 8
BD_RG = W_RG // H_RG
RG_CONV_WIDTH = 4
RG_C = 8.0
RG_A_MIN = 0.9
RG_A_MAX = 0.999
N_EXPERTS = 32
TOP_K = 4
D_FF = D_MODEL
SWIGLU_LIMIT = 7.0
SWIGLU_ALPHA = 1.702
LN_EPS = 1e-5
DEEPNORM_ALPHA = (2 * DEPTH) ** 0.25
DEEPNORM_BETA = (8 * DEPTH) ** -0.25
N_BRANCH = 3
OFF_Q = 0
OFF_K = OFF_Q + W_ATT
OFF_V = OFF_K + W_ATT
OFF_F = OFF_V + W_ATT
OFF_GLU = OFF_F + H_ATT
OFF_RGX = OFF_GLU + 2 * W_CONV
OFF_RGG = OFF_RGX + W_RG
OFF_GATE = OFF_RGG + W_RG
N_IN = OFF_GATE + N_BRANCH * D_MODEL

kernel_name = 'hybrid_fox_conformer_rglru_moe_step'


def _layer_norm(x, g, b):
    xf = x.astype(jnp.float32)
    mu = jnp.mean(xf, axis=-1, keepdims=True)
    var = jnp.mean(jnp.square(xf - mu), axis=-1, keepdims=True)
    y = (xf - mu) * lax.rsqrt(var + LN_EPS) * g.astype(jnp.float32) + b.astype(jnp.float32)
    return y.astype(x.dtype)


def _causal_dwconv(u, buf, w, b):
    width = w.shape[0]
    u_ext = jnp.concatenate([buf.astype(u.dtype), u], axis=1)
    y = lax.conv_general_dilated(u_ext, w[:, None, :].astype(u.dtype), window_strides=(1,), padding='VALID',
                                 dimension_numbers=('NWC', 'WIO', 'NWC'), feature_group_count=u.shape[-1])
    return y + b.astype(u.dtype), u_ext[:, u_ext.shape[1] - (width - 1):]


def _fox_attention(q, k, v, c, past):
    B, T, H, D = q.shape
    blk = Q_BLOCK if T % Q_BLOCK == 0 else T
    nb = T // blk
    k_pos = jnp.arange(k.shape[1])
    c_k = jnp.transpose(c, (0, 2, 1))
    q_b = q.reshape(B, nb, blk, H, D).transpose(1, 0, 2, 3, 4)
    cq_b = c_k[:, :, past:].reshape(B, H, nb, blk).transpose(2, 0, 1, 3)
    pos_b = (past + jnp.arange(T)).reshape(nb, blk)
    scale = D ** -0.5

    def one_block(args):
        qb, cqb, qpos = args
        s = jnp.einsum('bqhd,bkhd->bhqk', qb, k, preferred_element_type=jnp.float32) * scale
        s = s + (cqb[..., :, None] - c_k[:, :, None, :])
        s = jnp.where(k_pos[None, None, None, :] <= qpos[None, None, :, None], s, -jnp.inf)
        p = jax.nn.softmax(s, axis=-1)
        return jnp.einsum('bhqk,bkhd->bqhd', p.astype(v.dtype), v)

    o = lax.map(one_block, (q_b, cq_b, pos_b))
    return o.transpose(1, 0, 2, 3, 4).reshape(B, T, H, D)


def _rg_lru(xc, w_a, b_a, w_x, b_x, lam, h0):
    B, T, C = xc.shape
    xh = xc.reshape(B, T, H_RG, BD_RG)
    r = jax.nn.sigmoid(jnp.einsum('bthi,hij->bthj', xh, w_a).reshape(B, T, C) + b_a)
    i = jax.nn.sigmoid(jnp.einsum('bthi,hij->bthj', xh, w_x).reshape(B, T, C) + b_x)
    log_a = -RG_C * r.astype(jnp.float32) * jax.nn.softplus(-lam.astype(jnp.float32))
    a = jnp.exp(log_a)
    u = jnp.sqrt(-jnp.expm1(2.0 * log_a)) * (i * xc).astype(jnp.float32)

    def step(h, au):
        h = au[0] * h + au[1]
        return h, h

    h_last, hs = lax.scan(step, h0.astype(jnp.float32), (jnp.swapaxes(a, 0, 1), jnp.swapaxes(u, 0, 1)))
    return jnp.swapaxes(hs, 0, 1), h_last


def _moe(x, w_router, b_router, w_gate_up, b_gate_up, w_down, b_down):
    B, T, D = x.shape
    xt = x.reshape(B * T, D)
    logits = (xt @ w_router + b_router).astype(jnp.float32)
    top_val, top_idx = lax.top_k(logits, TOP_K)
    top_w = jax.nn.softmax(top_val, axis=-1)
    comb = jnp.sum(jax.nn.one_hot(top_idx, N_EXPERTS, dtype=jnp.float32) * top_w[..., None], axis=1)
    comb = comb.astype(x.dtype)
    y = jnp.zeros_like(xt)
    for e in range(N_EXPERTS):
        gu = xt @ w_gate_up[e] + b_gate_up[e]
        gate = jnp.minimum(gu[:, :D_FF], SWIGLU_LIMIT)
        up = jnp.clip(gu[:, D_FF:], -SWIGLU_LIMIT, SWIGLU_LIMIT)
        hid = (up + 1.0) * gate * jax.nn.sigmoid(SWIGLU_ALPHA * gate)
        y = y + comb[:, e:e + 1] * (hid @ w_down[e] + b_down[e])
    return y.reshape(B, T, D)


def _trunk_layer(x, k_past, v_past, logf_past, convb_buf, convc_buf, h0, lp):
    (w_in, b_f, w_att_o, conv_b_w, conv_b_b, lnb_g, lnb_b, w_conv_o, b_conv_o,
     conv_c_w, conv_c_b, w_rg_a, b_rg_a, w_rg_x, b_rg_x, rg_lambda, w_rg_o,
     w_out, ln1_g, ln1_b, w_router, b_router, w_gate_up, b_gate_up, w_down, b_down, ln2_g, ln2_b) = lp
    B, T, _ = x.shape
    z = x @ w_in
    q = z[..., OFF_Q:OFF_K].reshape(B, T, H_ATT, HD_ATT)
    k = z[..., OFF_K:OFF_V].reshape(B, T, H_ATT, HD_ATT)
    v = z[..., OFF_V:OFF_F].reshape(B, T, H_ATT, HD_ATT)
    logf = jax.nn.log_sigmoid(z[..., OFF_F:OFF_GLU].astype(jnp.float32) + b_f.astype(jnp.float32))
    glu_in = z[..., OFF_GLU:OFF_RGX]
    rg_x = z[..., OFF_RGX:OFF_RGG]
    rg_g = z[..., OFF_RGG:OFF_GATE]
    gates = jax.nn.sigmoid(z[..., OFF_GATE:].reshape(B, T, N_BRANCH, D_MODEL))

    past = k_past.shape[1]
    k_all = jnp.concatenate([k_past.astype(k.dtype), k], axis=1)
    v_all = jnp.concatenate([v_past.astype(v.dtype), v], axis=1)
    c = jnp.cumsum(jnp.concatenate([logf_past.astype(jnp.float32), logf], axis=1), axis=1)
    o_att = _fox_attention(q, k_all, v_all, c, past).reshape(B, T, W_ATT)
    y_a = o_att @ w_att_o

    u = glu_in[..., :W_CONV] * jax.nn.sigmoid(glu_in[..., W_CONV:])
    u_conv, convb_new = _causal_dwconv(u, convb_buf, conv_b_w, conv_b_b)
    u_conv = jax.nn.silu(_layer_norm(u_conv, lnb_g, lnb_b))
    y_b = u_conv @ w_conv_o + b_conv_o

    xc, convc_new = _causal_dwconv(rg_x, convc_buf, conv_c_w, conv_c_b)
    hs, h_last = _rg_lru(xc, w_rg_a, b_rg_a, w_rg_x, b_rg_x, rg_lambda, h0)
    y_c = (hs.astype(x.dtype) * jax.nn.gelu(rg_g)) @ w_rg_o

    merged = gates[:, :, 0] * y_a + gates[:, :, 1] * y_b + gates[:, :, 2] * y_c
    x = _layer_norm(DEEPNORM_ALPHA * x + merged @ w_out, ln1_g, ln1_b)
    x = _layer_norm(DEEPNORM_ALPHA * x + _moe(x, w_router, b_router, w_gate_up, b_gate_up, w_down, b_down),
                    ln2_g, ln2_b)
    return x, k, v, logf, convb_new, convc_new, h_last


def setup_inputs(seed: int = 0) -> dict:
    key = jax.random.key(seed)
    ks = iter(jax.random.split(key, 48))

    def nrm(shape, scale):
        return jax.random.normal(next(ks), shape, jnp.float32) * scale

    n_pages = PAST_LEN // PAGE_SIZE
    n_pool = (DEC_BATCH * n_pages * 5) // 4
    x_prompt = nrm((BATCH, SEQ, D_MODEL), 1.0)
    x_sample = nrm((DEC_BATCH, DEC_SEQ, D_MODEL), 1.0)
    cache_k = nrm((DEPTH, n_pool, PAGE_SIZE, H_ATT, HD_ATT), 1.0)
    cache_v = nrm((DEPTH, n_pool, PAGE_SIZE, H_ATT, HD_ATT), 1.0)
    cache_logf = jax.nn.log_sigmoid(FORGET_BIAS_INIT + nrm((DEPTH, n_pool, PAGE_SIZE, H_ATT), 1.0))
    state_conv_b = nrm((DEPTH, DEC_BATCH, CONV_WIDTH - 1, W_CONV), 0.5)
    state_conv_c = nrm((DEPTH, DEC_BATCH, RG_CONV_WIDTH - 1, W_RG), 1.0)
    state_h = nrm((DEPTH, DEC_BATCH, W_RG), 0.5)
    page_table = jax.random.permutation(next(ks), n_pool)[: DEC_BATCH * n_pages]
    page_table = page_table.reshape(DEC_BATCH, n_pages).astype(jnp.int32)

    w_in = nrm((DEPTH, D_MODEL, N_IN), D_MODEL ** -0.5)
    b_f = FORGET_BIAS_INIT + nrm((DEPTH, H_ATT), 0.1)
    w_att_o = nrm((DEPTH, W_ATT, D_MODEL), W_ATT ** -0.5)
    conv_b_w = nrm((DEPTH, CONV_WIDTH, W_CONV), CONV_WIDTH ** -0.5)
    conv_b_b = nrm((DEPTH, W_CONV), 0.02)
    lnb_g = 1.0 + nrm((DEPTH, W_CONV), 0.02)
    lnb_b = nrm((DEPTH, W_CONV), 0.02)
    w_conv_o = nrm((DEPTH, W_CONV, D_MODEL), W_CONV ** -0.5)
    b_conv_o = nrm((DEPTH, D_MODEL), 0.02)
    conv_c_w = nrm((DEPTH, RG_CONV_WIDTH, W_RG), RG_CONV_WIDTH ** -0.5)
    conv_c_b = nrm((DEPTH, W_RG), 0.02)
    w_rg_a = nrm((DEPTH, H_RG, BD_RG, BD_RG), BD_RG ** -0.5)
    b_rg_a = nrm((DEPTH, W_RG), 0.02)
    w_rg_x = nrm((DEPTH, H_RG, BD_RG, BD_RG), BD_RG ** -0.5)
    b_rg_x = nrm((DEPTH, W_RG), 0.02)
    a0 = jax.random.uniform(next(ks), (DEPTH, W_RG), jnp.float32, RG_A_MIN, RG_A_MAX)
    s0 = a0 ** (1.0 / RG_C)
    rg_lambda = jnp.log(s0) - jnp.log1p(-s0)
    w_rg_o = nrm((DEPTH, W_RG, D_MODEL), W_RG ** -0.5)
    w_out = nrm((DEPTH, D_MODEL, D_MODEL), D_MODEL ** -0.5 * DEEPNORM_BETA)
    ln1_g = 1.0 + nrm((DEPTH, D_MODEL), 0.02)
    ln1_b = nrm((DEPTH, D_MODEL), 0.02)
    w_router = nrm((DEPTH, D_MODEL, N_EXPERTS), D_MODEL ** -0.5)
    b_router = nrm((DEPTH, N_EXPERTS), 0.01)
    w_gate_up = nrm((DEPTH, N_EXPERTS, D_MODEL, 2 * D_FF), D_MODEL ** -0.5)
    b_gate_up = nrm((DEPTH, N_EXPERTS, 2 * D_FF), 0.02)
    w_down = nrm((DEPTH, N_EXPERTS, D_FF, D_MODEL), D_FF ** -0.5 * DEEPNORM_BETA)
    b_down = nrm((DEPTH, N_EXPERTS, D_MODEL), 0.02)
    ln2_g = 1.0 + nrm((DEPTH, D_MODEL), 0.02)
    ln2_b = nrm((DEPTH, D_MODEL), 0.02)
    return {'x_prompt': x_prompt, 'x_sample': x_sample, 'cache_k': cache_k, 'cache_v': cache_v,
            'cache_logf': cache_logf, 'state_conv_b': state_conv_b, 'state_conv_c': state_conv_c,
            'state_h': state_h, 'page_table': page_table,
            'w_in': w_in, 'b_f': b_f, 'w_att_o': w_att_o, 'conv_b_w': conv_b_w, 'conv_b_b': conv_b_b,
            'lnb_g': lnb_g, 'lnb_b': lnb_b, 'w_conv_o': w_conv_o, 'b_conv_o': b_conv_o,
            'conv_c_w': conv_c_w, 'conv_c_b': conv_c_b, 'w_rg_a': w_rg_a, 'b_rg_a': b_rg_a,
            'w_rg_x': w_rg_x, 'b_rg_x': b_rg_x, 'rg_lambda': rg_lambda, 'w_rg_o': w_rg_o,
            'w_out': w_out, 'ln1_g': ln1_g, 'ln1_b': ln1_b, 'w_router': w_router, 'b_router': b_router,
            'w_gate_up': w_gate_up, 'b_gate_up': b_gate_up, 'w_down': w_down, 'b_down': b_down,
            'ln2_g': ln2_g, 'ln2_b': ln2_b}


def reference(x_prompt, x_sample, cache_k, cache_v, cache_logf, state_conv_b, state_conv_c, state_h,
              page_table, w_in, b_f, w_att_o, conv_b_w, conv_b_b, lnb_g, lnb_b, w_conv_o, b_conv_o,
              conv_c_w, conv_c_b, w_rg_a, b_rg_a, w_rg_x, b_rg_x, rg_lambda, w_rg_o,
              w_out, ln1_g, ln1_b, w_router, b_router, w_gate_up, b_gate_up, w_down, b_down, ln2_g, ln2_b):
    b_p = x_prompt.shape[0]
    b_s = x_sample.shape[0]
    dt = x_prompt.dtype
    xp = x_prompt
    xs = x_sample
    kp_l, vp_l, fp_l, cbp_l, ccp_l, hp_l = [], [], [], [], [], []
    ks_l, vs_l, fs_l, cbs_l, ccs_l, hs_l = [], [], [], [], [], []
    for l in range(DEPTH):
        lp = (w_in[l], b_f[l], w_att_o[l], conv_b_w[l], conv_b_b[l], lnb_g[l], lnb_b[l], w_conv_o[l], b_conv_o[l],
              conv_c_w[l], conv_c_b[l], w_rg_a[l], b_rg_a[l], w_rg_x[l], b_rg_x[l], rg_lambda[l], w_rg_o[l],
              w_out[l], ln1_g[l], ln1_b[l], w_router[l], b_router[l], w_gate_up[l], b_gate_up[l],
              w_down[l], b_down[l], ln2_g[l], ln2_b[l])
        xp, kp, vp, fp, cbp, ccp, hp = _trunk_layer(
            xp,
            jnp.zeros((b_p, 0, H_ATT, HD_ATT), dt), jnp.zeros((b_p, 0, H_ATT, HD_ATT), dt),
            jnp.zeros((b_p, 0, H_ATT), jnp.float32),
            jnp.zeros((b_p, CONV_WIDTH - 1, W_CONV), dt), jnp.zeros((b_p, RG_CONV_WIDTH - 1, W_RG), dt),
            jnp.zeros((b_p, W_RG), jnp.float32), lp)
        k_past = cache_k[l, page_table].reshape(b_s, -1, H_ATT, HD_ATT)
        v_past = cache_v[l, page_table].reshape(b_s, -1, H_ATT, HD_ATT)
        f_past = cache_logf[l, page_table].reshape(b_s, -1, H_ATT)
        xs, k_s, v_s, f_s, cbs, ccs, h_s = _trunk_layer(
            xs, k_past, v_past, f_past, state_conv_b[l], state_conv_c[l], state_h[l], lp)
        kp_l.append(kp); vp_l.append(vp); fp_l.append(fp); cbp_l.append(cbp); ccp_l.append(ccp); hp_l.append(hp)
        ks_l.append(k_s); vs_l.append(v_s); fs_l.append(f_s); cbs_l.append(cbs); ccs_l.append(ccs); hs_l.append(h_s)
    y_prompt = xp
    y_sample = xs
    k_prompt = jnp.stack(kp_l, 0)
    v_prompt = jnp.stack(vp_l, 0)
    logf_prompt = jnp.stack(fp_l, 0)
    k_sample = jnp.stack(ks_l, 0)
    v_sample = jnp.stack(vs_l, 0)
    logf_sample = jnp.stack(fs_l, 0)
    conv_b_prompt = jnp.stack(cbp_l, 0)
    conv_b_sample = jnp.stack(cbs_l, 0)
    conv_c_prompt = jnp.stack(ccp_l, 0)
    conv_c_sample = jnp.stack(ccs_l, 0)
    h_prompt = jnp.stack(hp_l, 0)
    h_sample = jnp.stack(hs_l, 0)
    return (y_prompt, y_sample, k_prompt, v_prompt, logf_prompt, k_sample, v_sample, logf_sample,
            conv_b_prompt, conv_b_sample, conv_c_prompt, conv_c_sample, h_prompt, h_sample)
```

```python
import functools
import math

import jax
import jax.numpy as jnp
from jax import lax
from jax.experimental import pallas as pl
from jax.experimental.pallas import tpu as pltpu

F32 = jnp.float32
BF16 = jnp.bfloat16

H_ATT = 8
HD_ATT = 64
W_ATT = H_ATT * HD_ATT
CONV_WIDTH = 31
RG_CONV_WIDTH = 4
H_RG = 8
RG_C = 8.0
N_EXPERTS = 32
TOP_K = 4
SWIGLU_LIMIT = 7.0
SWIGLU_ALPHA = 1.702
LN_EPS = 1e-5
N_BRANCH = 3

LANES = 128
SUBLANES = 8
NEG_BIG = -1e30

TOKEN_TILE = 256
ATT_TILE = 512
CUMSUM_TILE = 512
SEQ_TILE = 256
CONVB_HALO = 32
CONVC_HALO = 8
CONV_CHUNK = 32
PAGES_PER_STEP = 8
MOE_TILE = 256


def _cparams(sem, vmem_mb):
    return pltpu.CompilerParams(dimension_semantics=sem, vmem_limit_bytes=vmem_mb << 20)


def _const_spec(shape):
    nd = len(shape)
    return pl.BlockSpec(shape, lambda *_: (0,) * nd)


def _log_sigmoid(z):
    return jnp.minimum(z, 0.0) - jnp.log1p(jnp.exp(-jnp.abs(z)))


def _softplus(z):
    return jnp.maximum(z, 0.0) + jnp.log1p(jnp.exp(-jnp.abs(z)))


def _layer_norm(x, g, b):
    mu = jnp.mean(x, axis=-1, keepdims=True)
    xc = x - mu
    var = jnp.mean(xc * xc, axis=-1, keepdims=True)
    return xc * lax.rsqrt(var + LN_EPS) * g + b


def _dot(a, b):
    return jnp.dot(a, b, preferred_element_type=F32)


def _dot_nt(a, b, precision=None):
    return lax.dot_general(a, b, (((1,), (1,)), ((), ())), preferred_element_type=F32, precision=precision)


def _inproj_body(x_ref, wqkv_ref, wf_ref, wft_ref, bf_ref, bft_ref, wglu_ref, wrg_ref, wgate_ref,
                 q_ref, k_ref, v_ref, kb_ref, vb_ref, lf_ref, lft_ref, u_ref, rgx_ref, gg_ref, gates_ref):
    xb = x_ref[...].astype(BF16)
    w = W_ATT
    qkv = _dot(xb, wqkv_ref[...])
    q_ref[...] = (qkv[:, :w] * (HD_ATT ** -0.5)).astype(BF16)
    k = qkv[:, w:2 * w]
    v = qkv[:, 2 * w:]
    k_ref[...] = k
    v_ref[...] = v
    kb_ref[...] = k.astype(BF16)
    vb_ref[...] = v.astype(BF16)
    zf = _dot(xb, wf_ref[...])
    lf_ref[...] = _log_sigmoid(zf[:, :H_ATT] + bf_ref[...])
    lft_ref[...] = _log_sigmoid(_dot_nt(wft_ref[...], xb) + bft_ref[...])
    glu = _dot(xb, wglu_ref[...])
    c = glu.shape[1] // 2
    u_ref[...] = glu[:, :c] * jax.nn.sigmoid(glu[:, c:])
    rg = _dot(xb, wrg_ref[...])
    c = rg.shape[1] // 2
    rgx_ref[...] = rg[:, :c]
    gg_ref[...] = jax.nn.gelu(rg[:, c:])
    d = gates_ref.shape[1] // N_BRANCH
    for j in range(N_BRANCH):
        gates_ref[:, j * d:(j + 1) * d] = jax.nn.sigmoid(_dot(xb, wgate_ref[:, j * d:(j + 1) * d]))


def _inproj(x, wts):
    n, d = x.shape
    tm = TOKEN_TILE
    wqkv, wf, wft, bf, bft, wglu, wrg, wgate = wts
    wc, wr = wglu.shape[1] // 2, wrg.shape[1] // 2

    def row(c):
        return pl.BlockSpec((tm, c), lambda i: (i, 0))

    out_shape = (
        jax.ShapeDtypeStruct((n, W_ATT), BF16), jax.ShapeDtypeStruct((n, W_ATT), F32),
        jax.ShapeDtypeStruct((n, W_ATT), F32), jax.ShapeDtypeStruct((n, W_ATT), BF16),
        jax.ShapeDtypeStruct((n, W_ATT), BF16), jax.ShapeDtypeStruct((n, H_ATT), F32),
        jax.ShapeDtypeStruct((H_ATT, n), F32), jax.ShapeDtypeStruct((n, wc), F32),
        jax.ShapeDtypeStruct((n, wr), F32), jax.ShapeDtypeStruct((n, wr), F32),
        jax.ShapeDtypeStruct((n, wgate.shape[1]), F32))
    out_specs = (row(W_ATT), row(W_ATT), row(W_ATT), row(W_ATT), row(W_ATT), row(H_ATT),
                 pl.BlockSpec((H_ATT, tm), lambda i: (0, i)), row(wc), row(wr), row(wr), row(wgate.shape[1]))
    return pl.pallas_call(
        _inproj_body, grid=(n // tm,),
        in_specs=[row(d)] + [_const_spec(a.shape) for a in wts],
        out_specs=out_specs, out_shape=out_shape,
        compiler_params=_cparams(("parallel",), 56), name="inproj")(x, *wts)


def _lane_cumsum(c):
    n = c.shape[1]
    lane = lax.broadcasted_iota(jnp.int32, c.shape, 1)
    d = 1
    while d < n:
        c = c + jnp.where(lane >= d, pltpu.roll(c, d, axis=1), 0.0)
        d *= 2
    return c


def _cumsum_body(f_ref, c_ref, carry_ref, *, blocks_per_seq):
    @pl.when(pl.program_id(0) % blocks_per_seq == 0)
    def _():
        carry_ref[...] = jnp.zeros_like(carry_ref)

    c = _lane_cumsum(f_ref[...]) + carry_ref[:, 0:1]
    c_ref[...] = c
    carry_ref[...] = jnp.broadcast_to(c[:, c.shape[1] - 1:], carry_ref.shape)


def _cumsum(ft, seq):
    h, n = ft.shape
    tc = min(CUMSUM_TILE, seq)
    return pl.pallas_call(
        functools.partial(_cumsum_body, blocks_per_seq=seq // tc), grid=(n // tc,),
        in_specs=[pl.BlockSpec((h, tc), lambda i: (0, i))],
        out_specs=pl.BlockSpec((h, tc), lambda i: (0, i)),
        out_shape=jax.ShapeDtypeStruct((h, n), F32),
        scratch_shapes=[pltpu.VMEM((h, LANES), F32)],
        compiler_params=_cparams(("arbitrary",), 16), name="logf_cumsum")(ft)


def _fox_prompt_body(q_ref, k_ref, v_ref, ck_ref, cq_ref, o_ref, m_ref, l_ref, acc_ref):
    qi = pl.program_id(1)
    ki = pl.program_id(2)
    tq, tk = q_ref.shape[0], k_ref.shape[0]

    @pl.when(ki == 0)
    def _():
        m_ref[...] = jnp.full_like(m_ref, NEG_BIG)
        l_ref[...] = jnp.zeros_like(l_ref)
        acc_ref[...] = jnp.zeros_like(acc_ref)

    def update(diagonal):
        if diagonal:
            row = lax.broadcasted_iota(jnp.int32, (tq, tk), 0)
            col = lax.broadcasted_iota(jnp.int32, (tq, tk), 1)
            keep = col <= row
        for h in range(H_ATT):
            sl = slice(h * HD_ATT, (h + 1) * HD_ATT)
            s = _dot_nt(q_ref[:, sl], k_ref[:, sl])
            s = s + (cq_ref[h:h + 1, 0:1] - ck_ref[h:h + 1, :])
            if diagonal:
                s = jnp.where(keep, s, NEG_BIG)
            m_old = m_ref[h]
            m_new = jnp.maximum(m_old, jnp.max(s, axis=-1, keepdims=True))
            p = jnp.exp(s - m_new)
            alpha = jnp.exp(m_old - m_new)
            l_ref[h] = alpha * l_ref[h] + jnp.sum(p, axis=-1, keepdims=True)
            acc_ref[h] = alpha * acc_ref[h] + _dot(p.astype(BF16), v_ref[:, sl])
            m_ref[h] = m_new

    @pl.when(ki < qi)
    def _():
        update(False)

    @pl.when(ki == qi)
    def _():
        update(True)
        for h in range(H_ATT):
            o_ref[:, h * HD_ATT:(h + 1) * HD_ATT] = (acc_ref[h] / l_ref[h]).astype(o_ref.dtype)


def _fox_prompt(q, kb, vb, ct, batch, seq):
    n = q.shape[0]
    t = min(ATT_TILE, seq)
    nb = seq // t

    def qmap(b, qi, ki):
        return (b * nb + qi, 0)

    def kmap(b, qi, ki):
        return (b * nb + jnp.minimum(ki, qi), 0)

    return pl.pallas_call(
        _fox_prompt_body, grid=(batch, nb, nb),
        in_specs=[pl.BlockSpec((t, W_ATT), qmap), pl.BlockSpec((t, W_ATT), kmap), pl.BlockSpec((t, W_ATT), kmap),
                  pl.BlockSpec((H_ATT, t), lambda b, qi, ki: (0, b * nb + jnp.minimum(ki, qi))),
                  pl.BlockSpec((H_ATT, t), lambda b, qi, ki: (0, b * nb + qi))],
        out_specs=pl.BlockSpec((t, W_ATT), qmap),
        out_shape=jax.ShapeDtypeStruct((n, W_ATT), BF16),
        scratch_shapes=[pltpu.VMEM((H_ATT, t, 1), F32), pltpu.VMEM((H_ATT, t, 1), F32),
                        pltpu.VMEM((H_ATT, t, HD_ATT), F32)],
        compiler_params=_cparams(("parallel", "parallel", "arbitrary"), 40), name="fox_prompt")(q, kb, vb, ct, ct)


def _fox_sample_body(pt_ref, qbd_ref, *refs, n_pages_step, page, n_new):
    del pt_ref
    np_ = n_pages_step
    k_refs = refs[:np_]
    v_refs = refs[np_:2 * np_]
    f_refs = refs[2 * np_:3 * np_]
    kn_ref, vn_ref, fn_ref, o_ref, m_ref, l_ref, acc_ref, carry_ref, kb_ref, vb_ref = refs[3 * np_:]
    c = pl.program_id(1)
    rows = qbd_ref.shape[0]
    eye = (lax.broadcasted_iota(jnp.int32, (H_ATT, H_ATT), 0)
           == lax.broadcasted_iota(jnp.int32, (H_ATT, H_ATT), 1)).astype(F32)

    @pl.when(c == 0)
    def _():
        m_ref[...] = jnp.full_like(m_ref, NEG_BIG)
        l_ref[...] = jnp.zeros_like(l_ref)
        acc_ref[...] = jnp.zeros_like(acc_ref)
        carry_ref[...] = jnp.zeros_like(carry_ref)

    def decay_rows(f):
        ft = _dot_nt(eye, f, precision=lax.Precision.HIGHEST)
        ct = _lane_cumsum(ft) + carry_ref[:, 0:1]
        carry_ref[...] = jnp.broadcast_to(ct[:, ct.shape[1] - 1:], carry_ref.shape)
        return jnp.concatenate([jnp.broadcast_to(ct[h:h + 1, :], (n_new, ct.shape[1])) for h in range(H_ATT)], axis=0)

    def update(kb, vb, cexp, keep):
        s = _dot_nt(qbd_ref[...], kb) - cexp
        if keep is not None:
            s = jnp.where(keep, s, NEG_BIG)
        m_old = m_ref[...]
        m_new = jnp.maximum(m_old, jnp.max(s, axis=-1, keepdims=True))
        p = jnp.exp(s - m_new)
        alpha = jnp.exp(m_old - m_new)
        l_ref[...] = alpha * l_ref[...] + jnp.sum(p, axis=-1, keepdims=True)
        acc_ref[...] = alpha * acc_ref[...] + _dot(p.astype(BF16), vb)
        m_ref[...] = m_new

    for j in range(np_):
        kb_ref[j * page:(j + 1) * page, :] = k_refs[j][...].astype(BF16)
        vb_ref[j * page:(j + 1) * page, :] = v_refs[j][...].astype(BF16)
    f = jnp.concatenate([r[...] for r in f_refs], axis=0)
    update(kb_ref[...], vb_ref[...], decay_rows(f), None)

    @pl.when(c == pl.num_programs(1) - 1)
    def _():
        tok = lax.broadcasted_iota(jnp.int32, (rows, page), 0) % n_new
        key = lax.broadcasted_iota(jnp.int32, (rows, page), 1)
        update(kn_ref[...].astype(BF16), vn_ref[...].astype(BF16), decay_rows(fn_ref[...]), key <= tok)
        acc = acc_ref[...] / l_ref[...]
        for h in range(H_ATT):
            o_ref[:, h * HD_ATT:(h + 1) * HD_ATT] = acc[h * n_new:(h + 1) * n_new,
                                                        h * HD_ATT:(h + 1) * HD_ATT].astype(o_ref.dtype)


def _fox_sample(layer, page_table, qbd, cache_k, cache_v, cache_f, k_new, v_new, f_new, n_new):
    bsz, n_pages = page_table.shape
    page = cache_k.shape[1]
    np_ = PAGES_PER_STEP
    while n_pages % np_:
        np_ //= 2
    rows = qbd.shape[1]
    base = layer

    def page_spec(width, j):
        return pl.BlockSpec((None, page, width), lambda b, c, pt: (base + pt[b, c * np_ + j], 0, 0))

    def seq_spec(r, width):
        return pl.BlockSpec((None, r, width), lambda b, c, pt: (b, 0, 0))

    in_specs = ([seq_spec(rows, W_ATT)]
                + [page_spec(W_ATT, j) for j in range(np_)]
                + [page_spec(W_ATT, j) for j in range(np_)]
                + [page_spec(H_ATT, j) for j in range(np_)]
                + [seq_spec(page, W_ATT), seq_spec(page, W_ATT), seq_spec(page, H_ATT)])
    grid_spec = pltpu.PrefetchScalarGridSpec(
        num_scalar_prefetch=1, grid=(bsz, n_pages // np_), in_specs=in_specs,
        out_specs=seq_spec(n_new, W_ATT),
        scratch_shapes=[pltpu.VMEM((rows, 1), F32), pltpu.VMEM((rows, 1), F32), pltpu.VMEM((rows, W_ATT), F32),
                        pltpu.VMEM((H_ATT, LANES), F32),
                        pltpu.VMEM((np_ * page, W_ATT), BF16), pltpu.VMEM((np_ * page, W_ATT), BF16)])
    return pl.pallas_call(
        functools.partial(_fox_sample_body, n_pages_step=np_, page=page, n_new=n_new),
        grid_spec=grid_spec, out_shape=jax.ShapeDtypeStruct((bsz, n_new, W_ATT), BF16),
        compiler_params=_cparams(("parallel", "arbitrary"), 40), name="fox_sample")(
            page_table, qbd, *([cache_k] * np_), *([cache_v] * np_), *([cache_f] * np_), k_new, v_new, f_new)


def _convb_body(*refs, tm, has_prev):
    if has_prev:
        u_ref, prev_ref, hist0_ref, w_ref, b_ref, g_ref, beta_ref, o_ref, ext_ref = refs
        hist = jnp.where(pl.program_id(1) == 0, hist0_ref[...], prev_ref[...])
    else:
        u_ref, hist0_ref, w_ref, b_ref, g_ref, beta_ref, o_ref, ext_ref = refs
        hist = hist0_ref[...]
    halo = CONVB_HALO
    ext_ref[0:halo, :] = hist
    ext_ref[halo:halo + tm, :] = u_ref[...]
    lead = halo - (CONV_WIDTH - 1)
    ch = min(CONV_CHUNK, tm)
    for r0 in range(0, tm, ch):
        acc = jnp.broadcast_to(b_ref[...], (ch, b_ref.shape[1]))
        for k in range(CONV_WIDTH):
            acc = acc + w_ref[k:k + 1, :] * ext_ref[r0 + k + lead:r0 + k + lead + ch, :]
        y = _layer_norm(acc, g_ref[...], beta_ref[...])
        o_ref[r0:r0 + ch, :] = (y * jax.nn.sigmoid(y)).astype(o_ref.dtype)


def _convb(u, hist0, w, b, g, beta):
    bsz, t, c = u.shape
    tm = min(SEQ_TILE, t)
    nt = t // tm
    has_prev = nt > 1
    halo = CONVB_HALO
    r = tm // halo if has_prev else 1
    in_specs = [pl.BlockSpec((None, tm, c), lambda bi, i: (bi, i, 0))]
    args = [u]
    if has_prev:
        in_specs.append(pl.BlockSpec((None, halo, c), lambda bi, i: (bi, jnp.maximum(i * r - 1, 0), 0)))
        args.append(u)
    in_specs += [pl.BlockSpec((None, halo, c), lambda bi, i: (bi, 0, 0)),
                 _const_spec(w.shape), _const_spec(b.shape), _const_spec(g.shape), _const_spec(beta.shape)]
    args += [hist0, w, b, g, beta]
    return pl.pallas_call(
        functools.partial(_convb_body, tm=tm, has_prev=has_prev), grid=(bsz, nt),
        in_specs=in_specs, out_specs=pl.BlockSpec((None, tm, c), lambda bi, i: (bi, i, 0)),
        out_shape=jax.ShapeDtypeStruct((bsz, t, c), BF16),
        scratch_shapes=[pltpu.VMEM((halo + tm, c), F32)],
        compiler_params=_cparams(("parallel", "arbitrary"), 24), name="conformer_conv")(*args)


def _rglru_body(*refs, tm, has_prev):
    if has_prev:
        (x_ref, prev_ref, hist0_ref, gg_ref, h0_ref, w_ref, b_ref, wa_ref, ba_ref, wx_ref, bx_ref, lam_ref,
         o_ref, hl_ref, ext_ref, h_ref) = refs
        hist = jnp.where(pl.program_id(1) == 0, hist0_ref[...], prev_ref[...])
    else:
        (x_ref, hist0_ref, gg_ref, h0_ref, w_ref, b_ref, wa_ref, ba_ref, wx_ref, bx_ref, lam_ref,
         o_ref, hl_ref, ext_ref, h_ref) = refs
        hist = hist0_ref[...]

    @pl.when(pl.program_id(1) == 0)
    def _():
        h_ref[...] = jnp.broadcast_to(h0_ref[...], h_ref.shape)

    halo = CONVC_HALO
    ext_ref[0:halo, :] = hist
    ext_ref[halo:halo + tm, :] = x_ref[...]
    lead = halo - (RG_CONV_WIDTH - 1)
    xc = jnp.broadcast_to(b_ref[...], x_ref.shape)
    for k in range(RG_CONV_WIDTH):
        xc = xc + w_ref[k:k + 1, :] * ext_ref[k + lead:k + lead + tm, :]
    xcb = xc.astype(BF16)
    r = jax.nn.sigmoid(_dot(xcb, wa_ref[...]) + ba_ref[...])
    ig = jax.nn.sigmoid(_dot(xcb, wx_ref[...]) + bx_ref[...])
    log_a = -RG_C * r * _softplus(-lam_ref[...])
    a = jnp.exp(log_a)
    u = jnp.sqrt(-jnp.tanh(log_a) * (1.0 + a * a)) * (ig * xc)
    row = lax.broadcasted_iota(jnp.int32, a.shape, 0)
    d = 1
    while d < tm:
        a_sh = pltpu.roll(a, d, axis=0)
        u_sh = pltpu.roll(u, d, axis=0)
        keep = row >= d
        u = jnp.where(keep, a * u_sh + u, u)
        a = jnp.where(keep, a * a_sh, a)
        d *= 2
    h = a * h_ref[0:1, :] + u
    o_ref[...] = (h * gg_ref[...]).astype(o_ref.dtype)
    last = h[tm - 1:tm, :]
    h_ref[...] = jnp.broadcast_to(last, h_ref.shape)
    hl_ref[...] = last


def _rglru(x, gg, hist0, h0, w, b, wa, ba, wx, bx, lam):
    bsz, t, c = x.shape
    tm = min(SEQ_TILE, t)
    nt = t // tm
    has_prev = nt > 1
    halo = CONVC_HALO
    r = tm // halo if has_prev else 1
    tile = pl.BlockSpec((None, tm, c), lambda bi, i: (bi, i, 0))
    in_specs = [tile]
    args = [x]
    if has_prev:
        in_specs.append(pl.BlockSpec((None, halo, c), lambda bi, i: (bi, jnp.maximum(i * r - 1, 0), 0)))
        args.append(x)
    in_specs += [pl.BlockSpec((None, halo, c), lambda bi, i: (bi, 0, 0)), tile,
                 pl.BlockSpec((None, 1, c), lambda bi, i: (bi, 0, 0))]
    args += [hist0, gg, h0]
    consts = [w, b, wa, ba, wx, bx, lam]
    in_specs += [_const_spec(a.shape) for a in consts]
    args += consts
    return pl.pallas_call(
        functools.partial(_rglru_body, tm=tm, has_prev=has_prev), grid=(bsz, nt),
        in_specs=in_specs,
        out_specs=(tile, pl.BlockSpec((None, 1, c), lambda bi, i: (bi, 0, 0))),
        out_shape=(jax.ShapeDtypeStruct((bsz, t, c), BF16), jax.ShapeDtypeStruct((bsz, 1, c), F32)),
        scratch_shapes=[pltpu.VMEM((halo + tm, c), F32), pltpu.VMEM((SUBLANES, c), F32)],
        compiler_params=_cparams(("parallel", "arbitrary"), 24), name="rglru")(*args)


def _merge_body(x_ref, oa_ref, ub_ref, hc_ref, gates_ref, wa_ref, wb_ref, bb_ref, wc_ref, wo_ref, g_ref, b_ref,
                wr_ref, br_ref, x1_ref, x1b_ref, idx_ref, tw_ref, *, alpha):
    d = x_ref.shape[1]
    ya = _dot(oa_ref[...], wa_ref[...])
    yb = _dot(ub_ref[...], wb_ref[...]) + bb_ref[...]
    yc = _dot(hc_ref[...], wc_ref[...])
    merged = gates_ref[:, 0:d] * ya + gates_ref[:, d:2 * d] * yb + gates_ref[:, 2 * d:3 * d] * yc
    x1 = _layer_norm(alpha * x_ref[...] + _dot(merged.astype(BF16), wo_ref[...]), g_ref[...], b_ref[...])
    x1_ref[...] = x1
    x1b_ref[...] = x1.astype(BF16)
    logits = jnp.dot(x1, wr_ref[...], preferred_element_type=F32, precision=lax.Precision.HIGHEST) + br_ref[...]
    lane = lax.broadcasted_iota(jnp.int32, logits.shape, 1)
    vals = logits
    idx_out = jnp.zeros(logits.shape, jnp.int32)
    val_out = jnp.full(logits.shape, NEG_BIG, F32)
    for k in range(TOP_K):
        m = jnp.max(vals, axis=-1, keepdims=True)
        idx = jnp.min(jnp.where(vals == m, lane, LANES), axis=-1, keepdims=True)
        idx_out = jnp.where(lane == k, idx, idx_out)
        val_out = jnp.where(lane == k, m, val_out)
        vals = jnp.where(lane == idx, -jnp.inf, vals)
    e = jnp.exp(val_out - jnp.max(val_out, axis=-1, keepdims=True))
    e = jnp.where(lane < TOP_K, e, 0.0)
    idx_ref[...] = idx_out
    tw_ref[...] = e / jnp.sum(e, axis=-1, keepdims=True)


def _merge(x, oa, ub, hc, gates, wts, alpha):
    n, d = x.shape
    tm = TOKEN_TILE

    def row(c):
        return pl.BlockSpec((tm, c), lambda i: (i, 0))

    return pl.pallas_call(
        functools.partial(_merge_body, alpha=alpha), grid=(n // tm,),
        in_specs=[row(d), row(oa.shape[1]), row(ub.shape[1]), row(hc.shape[1]), row(gates.shape[1])]
        + [_const_spec(a.shape) for a in wts],
        out_specs=(row(d), row(d), row(LANES), row(LANES)),
        out_shape=(jax.ShapeDtypeStruct((n, d), F32), jax.ShapeDtypeStruct((n, d), BF16),
                   jax.ShapeDtypeStruct((n, LANES), jnp.int32), jax.ShapeDtypeStruct((n, LANES), F32)),
        compiler_params=_cparams(("parallel",), 40), name="merge_router")(x, oa, ub, hc, gates, *wts)


def _moe_body(te_ref, tf_ref, tv_ref, xs_ref, rw_ref, wgu_ref, bgu_ref, wd_ref, bd_ref, ys_ref, wgub_ref, wdb_ref):
    del te_ref
    i = pl.program_id(0)

    @pl.when(tf_ref[i] == 1)
    def _():
        wgub_ref[...] = wgu_ref[...].astype(BF16)
        wdb_ref[...] = wd_ref[...].astype(BF16)

    @pl.when(tv_ref[i] == 1)
    def _():
        dff = wd_ref.shape[0]
        gu = _dot(xs_ref[...], wgub_ref[...]) + bgu_ref[...]
        gate = jnp.minimum(gu[:, :dff], SWIGLU_LIMIT)
        up = jnp.clip(gu[:, dff:], -SWIGLU_LIMIT, SWIGLU_LIMIT)
        hid = (up + 1.0) * gate * jax.nn.sigmoid(SWIGLU_ALPHA * gate)
        out = _dot(hid.astype(BF16), wdb_ref[...]) + bd_ref[...]
        ys_ref[...] = out * rw_ref[...]

    @pl.when(tv_ref[i] == 0)
    def _():
        ys_ref[...] = jnp.zeros_like(ys_ref)


def _moe_experts(layer, tile_expert, tile_first, tile_valid, xs, row_w, w_gate_up, b_gate_up, w_down, b_down):
    r, d = xs.shape
    tm = MOE_TILE
    n_exp, _, dgu = w_gate_up.shape[1:]
    dff = w_down.shape[2]
    bgu = b_gate_up.reshape(-1, 1, dgu)
    bd = b_down.reshape(-1, 1, d)
    grid_spec = pltpu.PrefetchScalarGridSpec(
        num_scalar_prefetch=3, grid=(r // tm,),
        in_specs=[pl.BlockSpec((tm, d), lambda i, te, tf, tv: (i, 0)),
                  pl.BlockSpec((tm, 1), lambda i, te, tf, tv: (i, 0)),
                  pl.BlockSpec((None, None, d, dgu), lambda i, te, tf, tv: (layer, te[i], 0, 0)),
                  pl.BlockSpec((None, 1, dgu), lambda i, te, tf, tv: (layer * n_exp + te[i], 0, 0)),
                  pl.BlockSpec((None, None, dff, d), lambda i, te, tf, tv: (layer, te[i], 0, 0)),
                  pl.BlockSpec((None, 1, d), lambda i, te, tf, tv: (layer * n_exp + te[i], 0, 0))],
        out_specs=pl.BlockSpec((tm, d), lambda i, te, tf, tv: (i, 0)),
        scratch_shapes=[pltpu.VMEM((d, dgu), BF16), pltpu.VMEM((dff, d), BF16)])
    return pl.pallas_call(
        _moe_body, grid_spec=grid_spec, out_shape=jax.ShapeDtypeStruct((r, d), F32),
        compiler_params=_cparams(("arbitrary",), 56), name="moe_experts")(
            tile_expert, tile_first, tile_valid, xs, row_w, w_gate_up, bgu, w_down, bd)


def _combine_body(x1_ref, yg_ref, g_ref, b_ref, o_ref, *, alpha):
    y = yg_ref[0]
    for k in range(1, TOP_K):
        y = y + yg_ref[k]
    o_ref[...] = _layer_norm(alpha * x1_ref[...] + y, g_ref[...], b_ref[...])


def _combine(x1, yg, g, b, alpha):
    n, d = x1.shape
    tm = TOKEN_TILE
    return pl.pallas_call(
        functools.partial(_combine_body, alpha=alpha), grid=(n // tm,),
        in_specs=[pl.BlockSpec((tm, d), lambda i: (i, 0)), pl.BlockSpec((TOP_K, tm, d), lambda i: (0, i, 0)),
                  _const_spec(g.shape), _const_spec(b.shape)],
        out_specs=pl.BlockSpec((tm, d), lambda i: (i, 0)), out_shape=jax.ShapeDtypeStruct((n, d), F32),
        compiler_params=_cparams(("parallel",), 32), name="moe_combine_norm")(x1, yg, g, b)


def _dispatch(top_idx, top_w):
    n = top_idx.shape[0]
    tm = MOE_TILE
    a = n * TOP_K
    n_tiles = -(-a // tm) + N_EXPERTS
    e = top_idx.reshape(a)
    onehot = (e[:, None] == jnp.arange(N_EXPERTS, dtype=jnp.int32)[None, :]).astype(jnp.int32)
    cum = jnp.cumsum(onehot, axis=0)
    rank = jnp.sum(onehot * cum, axis=1) - 1
    cnt = cum[-1]
    ptiles = (cnt + tm - 1) // tm
    tile_end = jnp.cumsum(ptiles)
    pstart = (tile_end - ptiles) * tm
    pos = pstart[e] + rank
    row_token = jnp.zeros((n_tiles * tm,), jnp.int32).at[pos].set(jnp.arange(a, dtype=jnp.int32) // TOP_K)
    row_w = jnp.zeros((n_tiles * tm,), F32).at[pos].set(top_w.reshape(a))
    tiles = jnp.arange(n_tiles, dtype=jnp.int32)
    te = jnp.minimum(jnp.searchsorted(tile_end, tiles, side="right").astype(jnp.int32), N_EXPERTS - 1)
    tv = (tiles < tile_end[-1]).astype(jnp.int32)
    tf = jnp.concatenate([jnp.ones((1,), jnp.int32), (te[1:] != te[:-1]).astype(jnp.int32)])
    return row_token, row_w.reshape(-1, 1), pos.reshape(n, TOP_K), te, tf, tv


def _pad_rows(a, rows):
    return jnp.pad(a, ((0, 0), (rows - a.shape[1], 0), (0, 0)))


def _block_diag(w):
    h, bd, _ = w.shape
    eye = jnp.eye(h, dtype=w.dtype)
    return jnp.einsum("hij,hg->higj", w, eye).reshape(h * bd, h * bd)


def kernel(x_prompt, x_sample, cache_k, cache_v, cache_logf, state_conv_b, state_conv_c, state_h, page_table,
           w_in, b_f, w_att_o, conv_b_w, conv_b_b, lnb_g, lnb_b, w_conv_o, b_conv_o, conv_c_w, conv_c_b,
           w_rg_a, b_rg_a, w_rg_x, b_rg_x, rg_lambda, w_rg_o, w_out, ln1_g, ln1_b, w_router, b_router,
           w_gate_up, b_gate_up, w_down, b_down, ln2_g, ln2_b):
    bp, seq, d = x_prompt.shape
    bs, ns, _ = x_sample.shape
    depth, n_pool, page = cache_k.shape[:3]
    w_conv = state_conv_b.shape[-1]
    w_rg = state_conv_c.shape[-1]
    alpha = (2 * depth) ** 0.25
    n_p, n_s = bp * seq, bs * ns
    off_f = 3 * W_ATT
    off_glu = off_f + H_ATT
    off_rg = off_glu + 2 * w_conv
    off_gate = off_rg + 2 * w_rg

    ck = cache_k.reshape(depth * n_pool, page, W_ATT)
    cv = cache_v.reshape(depth * n_pool, page, W_ATT)
    cf = cache_logf.reshape(depth * n_pool, page, H_ATT)

    xp = x_prompt.reshape(n_p, d)
    xs = x_sample.reshape(n_s, d)
    outs = {k: [] for k in ("kp", "vp", "fp", "ks", "vs", "fs", "cbp", "cbs", "ccp", "ccs", "hp", "hs")}
    head_eye = jnp.eye(H_ATT, dtype=BF16)

    for l in range(depth):
        wl = w_in[l]
        wf = wl[:, off_f:off_glu]
        in_w = (wl[:, :off_f].astype(BF16),
                jnp.pad(wf, ((0, 0), (0, LANES - H_ATT))).astype(BF16), wf.T.astype(BF16),
                b_f[l].reshape(1, H_ATT), b_f[l].reshape(H_ATT, 1),
                wl[:, off_glu:off_rg].astype(BF16), wl[:, off_rg:off_gate].astype(BF16),
                wl[:, off_gate:].astype(BF16))
        cbw = jnp.pad(conv_b_w[l], ((0, CONVB_HALO - CONV_WIDTH), (0, 0)))
        ccw = jnp.pad(conv_c_w[l], ((0, CONVC_HALO - RG_CONV_WIDTH), (0, 0)))
        rg_w = (ccw, conv_c_b[l].reshape(1, -1), _block_diag(w_rg_a[l]).astype(BF16), b_rg_a[l].reshape(1, -1),
                _block_diag(w_rg_x[l]).astype(BF16), b_rg_x[l].reshape(1, -1), rg_lambda[l].reshape(1, -1))
        cb_w = (cbw, conv_b_b[l].reshape(1, -1), lnb_g[l].reshape(1, -1), lnb_b[l].reshape(1, -1))
        wr = jnp.pad(w_router[l], ((0, 0), (0, LANES - N_EXPERTS)))
        br = jnp.pad(b_router[l], (0, LANES - N_EXPERTS), constant_values=NEG_BIG).reshape(1, LANES)
        mg_w = (w_att_o[l].astype(BF16), w_conv_o[l].astype(BF16), b_conv_o[l].reshape(1, -1),
                w_rg_o[l].astype(BF16), w_out[l].astype(BF16), ln1_g[l].reshape(1, -1), ln1_b[l].reshape(1, -1),
                wr, br)

        (q, k, v, kb, vb, lf, lft, u, rgx, gg, gates) = _inproj(xp, in_w)
        ct = _cumsum(lft, seq)
        oa = _fox_prompt(q, kb, vb, ct, bp, seq)
        u3 = u.reshape(bp, seq, w_conv)
        ub = _convb(u3, jnp.zeros((bp, CONVB_HALO, w_conv), F32), *cb_w)
        rgx3 = rgx.reshape(bp, seq, w_rg)
        hc, hl = _rglru(rgx3, gg.reshape(bp, seq, w_rg), jnp.zeros((bp, CONVC_HALO, w_rg), F32),
                        jnp.zeros((bp, 1, w_rg), F32), *rg_w)
        x1p, x1bp, tip, twp = _merge(xp, oa, ub.reshape(n_p, w_conv), hc.reshape(n_p, w_rg), gates, mg_w, alpha)
        outs["kp"].append(k.reshape(bp, seq, H_ATT, HD_ATT))
        outs["vp"].append(v.reshape(bp, seq, H_ATT, HD_ATT))
        outs["fp"].append(lf.reshape(bp, seq, H_ATT))
        outs["cbp"].append(u3[:, seq - (CONV_WIDTH - 1):])
        outs["ccp"].append(rgx3[:, seq - (RG_CONV_WIDTH - 1):])
        outs["hp"].append(hl.reshape(bp, w_rg))

        (q, k, v, kb, vb, lf, lft, u, rgx, gg, gates) = _inproj(xs, in_w)
        qbd = jnp.einsum("bthd,hg->bhtgd", q.reshape(bs, ns, H_ATT, HD_ATT), head_eye)
        qbd = qbd.reshape(bs, H_ATT * ns, W_ATT)
        pad_new = ((0, 0), (0, page - ns), (0, 0))
        oa = _fox_sample(l * n_pool, page_table, qbd, ck, cv, cf,
                         jnp.pad(k.reshape(bs, ns, W_ATT), pad_new), jnp.pad(v.reshape(bs, ns, W_ATT), pad_new),
                         jnp.pad(lf.reshape(bs, ns, H_ATT), pad_new), ns)
        u3 = u.reshape(bs, ns, w_conv)
        ub = _convb(u3, _pad_rows(state_conv_b[l], CONVB_HALO), *cb_w)
        rgx3 = rgx.reshape(bs, ns, w_rg)
        hc, hl = _rglru(rgx3, gg.reshape(bs, ns, w_rg), _pad_rows(state_conv_c[l], CONVC_HALO),
                        state_h[l].reshape(bs, 1, w_rg), *rg_w)
        x1s, x1bs, tis, tws = _merge(xs, oa.reshape(n_s, W_ATT), ub.reshape(n_s, w_conv), hc.reshape(n_s, w_rg),
                                     gates, mg_w, alpha)
        outs["ks"].append(k.reshape(bs, ns, H_ATT, HD_ATT))
        outs["vs"].append(v.reshape(bs, ns, H_ATT, HD_ATT))
        outs["fs"].append(lf.reshape(bs, ns, H_ATT))
        outs["cbs"].append(jnp.concatenate([state_conv_b[l], u3], axis=1)[:, ns:])
        outs["ccs"].append(jnp.concatenate([state_conv_c[l], rgx3], axis=1)[:, ns:])
        outs["hs"].append(hl.reshape(bs, w_rg))

        top_idx = jnp.concatenate([tip[:, :TOP_K], tis[:, :TOP_K]], axis=0)
        top_w = jnp.concatenate([twp[:, :TOP_K], tws[:, :TOP_K]], axis=0)
        row_token, row_w, pos, te, tf, tv = _dispatch(top_idx, top_w)
        x1b = jnp.concatenate([x1bp, x1bs], axis=0)
        ys = _moe_experts(l, te, tf, tv, jnp.take(x1b, row_token, axis=0), row_w,
                          w_gate_up, b_gate_up, w_down, b_down)
        g2, b2 = ln2_g[l].reshape(1, -1), ln2_b[l].reshape(1, -1)
        xp = _combine(x1p, jnp.take(ys, pos[:n_p].T, axis=0), g2, b2, alpha)
        xs = _combine(x1s, jnp.take(ys, pos[n_p:].T, axis=0), g2, b2, alpha)

    st = {k: jnp.stack(v, 0) for k, v in outs.items()}
    return (xp.reshape(bp, seq, d), xs.reshape(bs, ns, d), st["kp"], st["vp"], st["fp"], st["ks"], st["vs"],
            st["fs"], st["cbp"], st["cbs"], st["ccp"], st["ccs"], st["hp"], st["hs"])
```

```python
import functools
import math

import jax
import jax.numpy as jnp
from jax import lax
from jax.experimental import pallas as pl
from jax.experimental.pallas import tpu as pltpu
from jax.experimental.pallas import tpu_sc as plsc

F32 = jnp.float32
BF16 = jnp.bfloat16

H_ATT = 8
HD_ATT = 64
W_ATT = H_ATT * HD_ATT
CONV_WIDTH = 31
RG_CONV_WIDTH = 4
H_RG = 8
RG_C = 8.0
N_EXPERTS = 32
TOP_K = 4
SWIGLU_LIMIT = 7.0
SWIGLU_ALPHA = 1.702
LN_EPS = 1e-5
N_BRANCH = 3

LANES = 128
SUBLANES = 8
NEG_BIG = -1e30

TOKEN_TILE = 256
ATT_TILE = 512
CUMSUM_TILE = 512
SEQ_TILE = 256
CONVB_HALO = 32
CONVC_HALO = 8
CONV_CHUNK = 32
PAGES_PER_STEP = 8
MOE_TILE = 256
PLANE_WIDTH = 256
SC_WINDOW = 128
DISPATCH_TILES = (1280, 1024, 768, 512, 256)


def _cparams(sem, vmem_mb):
    return pltpu.CompilerParams(dimension_semantics=sem, vmem_limit_bytes=vmem_mb << 20)


def _const_spec(shape):
    nd = len(shape)
    return pl.BlockSpec(shape, lambda *_: (0,) * nd)


def _log_sigmoid(z):
    return jnp.minimum(z, 0.0) - jnp.log1p(jnp.exp(-jnp.abs(z)))


def _softplus(z):
    return jnp.maximum(z, 0.0) + jnp.log1p(jnp.exp(-jnp.abs(z)))


def _layer_norm(x, g, b):
    mu = jnp.mean(x, axis=-1, keepdims=True)
    xc = x - mu
    var = jnp.mean(xc * xc, axis=-1, keepdims=True)
    return xc * lax.rsqrt(var + LN_EPS) * g + b


def _dot(a, b):
    return jnp.dot(a, b, preferred_element_type=F32)


def _dot_nt(a, b, precision=None):
    return lax.dot_general(a, b, (((1,), (1,)), ((), ())), preferred_element_type=F32, precision=precision)


def _inproj_body(x_ref, wqkv_ref, wqt_ref, wvt_ref, wf_ref, bf_ref, wglu_ref, wrg_ref, wgate_ref,
                 q_ref, k_ref, v_ref, kb_ref, qtm_ref, vt_ref, lf_ref, u_ref, rgx_ref, gg_ref, gates_ref):
    xb = x_ref[...].astype(BF16)
    w = W_ATT
    scale = HD_ATT ** -0.5
    qkv = _dot(xb, wqkv_ref[...])
    q_ref[...] = (qkv[:, :w] * scale).astype(BF16)
    k = qkv[:, w:2 * w]
    k_ref[...] = k
    v_ref[...] = qkv[:, 2 * w:]
    kb_ref[...] = k.astype(BF16)
    vt_ref[...] = _dot_nt(wvt_ref[...], xb).astype(BF16)
    qt = (_dot_nt(wqt_ref[...], xb) * scale).astype(BF16)
    pair = 2 * HD_ATT
    lower = lax.broadcasted_iota(jnp.int32, (pair, qt.shape[1]), 0) < HD_ATT
    for h in range(H_ATT):
        slab = qt[(h // 2) * pair:(h // 2 + 1) * pair, :]
        qtm_ref[h] = jnp.where(lower if h % 2 == 0 else jnp.logical_not(lower), slab, jnp.zeros_like(slab))
    zf = _dot(xb, wf_ref[...])
    lf_ref[...] = _log_sigmoid(zf[:, :H_ATT] + bf_ref[...])
    glu = _dot(xb, wglu_ref[...])
    c = glu.shape[1] // 2
    u_ref[...] = glu[:, :c] * jax.nn.sigmoid(glu[:, c:])
    rg = _dot(xb, wrg_ref[...])
    c = rg.shape[1] // 2
    rgx_ref[...] = rg[:, :c]
    gg_ref[...] = jax.nn.gelu(rg[:, c:])
    d = gates_ref.shape[1] // N_BRANCH
    for j in range(N_BRANCH):
        gates_ref[:, j * d:(j + 1) * d] = jax.nn.sigmoid(_dot(xb, wgate_ref[:, j * d:(j + 1) * d]))


def _inproj(x, wts):
    n, d = x.shape
    tm = TOKEN_TILE
    wqkv, wqt, wvt, wf, bf, wglu, wrg, wgate = wts
    wc, wr = wglu.shape[1] // 2, wrg.shape[1] // 2

    def row(c):
        return pl.BlockSpec((tm, c), lambda i: (i, 0))

    out_shape = (
        jax.ShapeDtypeStruct((n, W_ATT), BF16), jax.ShapeDtypeStruct((n, W_ATT), F32),
        jax.ShapeDtypeStruct((n, W_ATT), F32), jax.ShapeDtypeStruct((n, W_ATT), BF16),
        jax.ShapeDtypeStruct((H_ATT, 2 * HD_ATT, n), BF16), jax.ShapeDtypeStruct((W_ATT, n), BF16),
        jax.ShapeDtypeStruct((n, H_ATT), F32), jax.ShapeDtypeStruct((n, wc), F32),
        jax.ShapeDtypeStruct((n, wr), F32), jax.ShapeDtypeStruct((n, wr), F32),
        jax.ShapeDtypeStruct((n, wgate.shape[1]), F32))
    out_specs = (row(W_ATT), row(W_ATT), row(W_ATT), row(W_ATT),
                 pl.BlockSpec((H_ATT, 2 * HD_ATT, tm), lambda i: (0, 0, i)),
                 pl.BlockSpec((W_ATT, tm), lambda i: (0, i)),
                 row(H_ATT), row(wc), row(wr), row(wr), row(wgate.shape[1]))
    return pl.pallas_call(
        _inproj_body, grid=(n // tm,),
        in_specs=[row(d)] + [_const_spec(a.shape) for a in wts],
        out_specs=out_specs, out_shape=out_shape,
        compiler_params=_cparams(("parallel",), 56), name="inproj")(x, *wts)


def _lane_cumsum(c):
    n = c.shape[1]
    lane = lax.broadcasted_iota(jnp.int32, c.shape, 1)
    d = 1
    while d < n:
        c = c + jnp.where(lane >= d, pltpu.roll(c, d, axis=1), 0.0)
        d *= 2
    return c


def _cumsum_body(f_ref, c_ref, carry_ref, *, blocks_per_seq):
    @pl.when(pl.program_id(0) % blocks_per_seq == 0)
    def _():
        carry_ref[...] = jnp.zeros_like(carry_ref)

    c = f_ref[...]
    n = c.shape[0]
    row = lax.broadcasted_iota(jnp.int32, c.shape, 0)
    d = 1
    while d < n:
        c = c + jnp.where(row >= d, pltpu.roll(c, d, axis=0), 0.0)
        d *= 2
    c = c + carry_ref[0:1, :]
    carry_ref[...] = jnp.broadcast_to(c[n - 1:n, :], carry_ref.shape)
    for h in range(H_ATT):
        c_ref[h] = jnp.broadcast_to(c[:, h:h + 1], (n, LANES))


def _cumsum(lf, seq):
    n, h = lf.shape
    tc = min(CUMSUM_TILE, seq)
    return pl.pallas_call(
        functools.partial(_cumsum_body, blocks_per_seq=seq // tc), grid=(n // tc,),
        in_specs=[pl.BlockSpec((tc, h), lambda i: (i, 0))],
        out_specs=pl.BlockSpec((h, tc, LANES), lambda i: (0, i, 0)),
        out_shape=jax.ShapeDtypeStruct((h, n, LANES), F32),
        scratch_shapes=[pltpu.VMEM((SUBLANES, h), F32)],
        compiler_params=_cparams(("arbitrary",), 16), name="logf_cumsum")(lf)


def _fox_prompt_body(qtm_ref, k_ref, vt_ref, ck_ref, cq_ref, o_ref, m_ref, l_ref, acc_ref):
    qi = pl.program_id(1)
    ki = pl.program_id(2)
    tk, tq = k_ref.shape[0], qtm_ref.shape[2]
    pair = 2 * HD_ATT

    @pl.when(ki == 0)
    def _():
        m_ref[...] = jnp.full_like(m_ref, NEG_BIG)
        l_ref[...] = jnp.zeros_like(l_ref)
        acc_ref[...] = jnp.zeros_like(acc_ref)

    def update(diagonal):
        if diagonal:
            keep = lax.broadcasted_iota(jnp.int32, (tk, tq), 0) <= lax.broadcasted_iota(jnp.int32, (tk, tq), 1)
        for h in range(H_ATT):
            s = _dot(k_ref[:, (h // 2) * pair:(h // 2 + 1) * pair], qtm_ref[h])
            bias = ck_ref[h] - cq_ref[h, 0:1, :]
            s = s - jnp.concatenate([bias] * (tq // LANES), axis=1)
            if diagonal:
                s = jnp.where(keep, s, NEG_BIG)
            m_old = m_ref[h]
            m_new = jnp.maximum(m_old, jnp.max(s, axis=0, keepdims=True))
            p = jnp.exp(s - m_new)
            alpha = jnp.exp(m_old - m_new)
            l_ref[h] = alpha * l_ref[h] + jnp.sum(p, axis=0, keepdims=True)
            acc_ref[h] = alpha * acc_ref[h] + _dot(vt_ref[h * HD_ATT:(h + 1) * HD_ATT, :], p.astype(BF16))
            m_ref[h] = m_new

    @pl.when(ki < qi)
    def _():
        update(False)

    @pl.when(ki == qi)
    def _():
        update(True)
        for j in range(H_ATT // 2):
            o2 = jnp.concatenate([acc_ref[2 * j] / l_ref[2 * j], acc_ref[2 * j + 1] / l_ref[2 * j + 1]], axis=0)
            o_ref[:, j * pair:(j + 1) * pair] = o2.T.astype(o_ref.dtype)


def _fox_prompt(qtm, kb, vt, cb, batch, seq):
    n = kb.shape[0]
    t = min(ATT_TILE, seq)
    nb = seq // t

    def kblk(b, qi, ki):
        return b * nb + jnp.minimum(ki, qi)

    return pl.pallas_call(
        _fox_prompt_body, grid=(batch, nb, nb),
        in_specs=[pl.BlockSpec((H_ATT, 2 * HD_ATT, t), lambda b, qi, ki: (0, 0, b * nb + qi)),
                  pl.BlockSpec((t, W_ATT), lambda b, qi, ki: (kblk(b, qi, ki), 0)),
                  pl.BlockSpec((W_ATT, t), lambda b, qi, ki: (0, kblk(b, qi, ki))),
                  pl.BlockSpec((H_ATT, t, LANES), lambda b, qi, ki: (0, kblk(b, qi, ki), 0)),
                  pl.BlockSpec((H_ATT, SUBLANES, LANES), lambda b, qi, ki: (0, (b * nb + qi) * (t // SUBLANES), 0))],
        out_specs=pl.BlockSpec((t, W_ATT), lambda b, qi, ki: (b * nb + qi, 0)),
        out_shape=jax.ShapeDtypeStruct((n, W_ATT), BF16),
        scratch_shapes=[pltpu.VMEM((H_ATT, 1, t), F32), pltpu.VMEM((H_ATT, 1, t), F32),
                        pltpu.VMEM((H_ATT, HD_ATT, t), F32)],
        compiler_params=_cparams(("parallel", "parallel", "arbitrary"), 40), name="fox_prompt")(
            qtm, kb, vt, cb, cb)


def _fox_sample_body(pt_ref, q_ref, *refs, n_pages_step, n_new):
    del pt_ref
    np_ = n_pages_step
    k_refs = refs[:np_]
    v_refs = refs[np_:2 * np_]
    f_refs = refs[2 * np_:3 * np_]
    kn_ref, vn_ref, fn_ref, o_ref, m_ref, l_ref, acc_ref, carry_ref, f_scr, kb_ref, vb_ref = refs[3 * np_:]
    c = pl.program_id(1)
    rows = q_ref.shape[0]
    flat = f_scr.shape[1]
    row_i = lax.broadcasted_iota(jnp.int32, (rows, flat), 0)
    col_i = lax.broadcasted_iota(jnp.int32, (rows, flat), 1)
    own_head = (col_i % H_ATT) == (row_i // n_new)

    @pl.when(c == 0)
    def _():
        m_ref[...] = jnp.full_like(m_ref, NEG_BIG)
        l_ref[...] = jnp.zeros_like(l_ref)
        acc_ref[...] = jnp.zeros_like(acc_ref)
        carry_ref[...] = jnp.zeros_like(carry_ref)

    def decay(f):
        n_pg = f.shape[0]
        lane = lax.broadcasted_iota(jnp.int32, f.shape, 1)
        within, total = f, f
        d = H_ATT
        while d < flat:
            within = within + jnp.where(lane >= d, pltpu.roll(within, d, axis=1), 0.0)
            total = total + pltpu.roll(total, d, axis=1)
            d *= 2
        pg = lax.broadcasted_iota(jnp.int32, f.shape, 0)
        before = jnp.zeros_like(f)
        for j in range(n_pg - 1):
            before = before + jnp.where(pg > j, total[j:j + 1, :], 0.0)
        cum = within + before + carry_ref[...]
        carry_ref[...] = carry_ref[...] + jnp.sum(total, axis=0, keepdims=True)
        return cum

    def update(n_pg, cum, keep):
        parts = []
        for j in range(n_pg):
            s = _dot_nt(q_ref[...], kb_ref[j * flat:(j + 1) * flat, :]) - cum[j:j + 1, :]
            parts.append(jnp.where(keep, s, NEG_BIG))
        m_old = m_ref[...]
        m_new = m_old
        for s in parts:
            m_new = jnp.maximum(m_new, jnp.max(s, axis=-1, keepdims=True))
        alpha = jnp.exp(m_old - m_new)
        l_new = alpha * l_ref[...]
        acc = alpha * acc_ref[...]
        for j, s in enumerate(parts):
            p = jnp.exp(s - m_new)
            l_new = l_new + jnp.sum(p, axis=-1, keepdims=True)
            acc = acc + _dot(p.astype(BF16), vb_ref[j * flat:(j + 1) * flat, :])
        l_ref[...] = l_new
        acc_ref[...] = acc
        m_ref[...] = m_new

    for j in range(np_):
        kb_ref[j * flat:(j + 1) * flat, :] = k_refs[j][...].reshape(flat, HD_ATT).astype(BF16)
        vb_ref[j * flat:(j + 1) * flat, :] = v_refs[j][...].reshape(flat, HD_ATT).astype(BF16)
        f_scr[j:j + 1, :] = f_refs[j][...]
    update(np_, decay(f_scr[...]), own_head)

    @pl.when(c == pl.num_programs(1) - 1)
    def _():
        kb_ref[0:flat, :] = kn_ref[...].reshape(flat, HD_ATT).astype(BF16)
        vb_ref[0:flat, :] = vn_ref[...].reshape(flat, HD_ATT).astype(BF16)
        visible = jnp.logical_and(own_head, (col_i // H_ATT) <= (row_i % n_new))
        update(1, decay(fn_ref[...]), visible)
        acc = acc_ref[...] / l_ref[...]
        for h in range(H_ATT):
            o_ref[:, h * HD_ATT:(h + 1) * HD_ATT] = acc[h * n_new:(h + 1) * n_new, :].astype(o_ref.dtype)


def _fox_sample(layer, page_table, q_rows, cache_k, cache_v, cache_f, k_new, v_new, f_new, n_new):
    bsz, n_pages = page_table.shape
    n_pool, page = cache_k.shape[1:3]
    flat = page * H_ATT
    np_ = PAGES_PER_STEP
    while n_pages % np_:
        np_ //= 2
    rows = q_rows.shape[1]

    def kv_spec(j):
        return pl.BlockSpec((None, None, page, H_ATT, HD_ATT), lambda b, c, pt: (layer, pt[b, c * np_ + j], 0, 0, 0))

    def f_spec(j):
        return pl.BlockSpec((None, 1, flat), lambda b, c, pt: (layer * n_pool + pt[b, c * np_ + j], 0, 0))

    new_kv = pl.BlockSpec((None, page, H_ATT, HD_ATT), lambda b, c, pt: (b, 0, 0, 0))
    in_specs = ([pl.BlockSpec((None, rows, HD_ATT), lambda b, c, pt: (b, 0, 0))]
                + [kv_spec(j) for j in range(np_)] + [kv_spec(j) for j in range(np_)]
                + [f_spec(j) for j in range(np_)]
                + [new_kv, new_kv, pl.BlockSpec((None, 1, flat), lambda b, c, pt: (b, 0, 0))])
    grid_spec = pltpu.PrefetchScalarGridSpec(
        num_scalar_prefetch=1, grid=(bsz, n_pages // np_), in_specs=in_specs,
        out_specs=pl.BlockSpec((None, n_new, W_ATT), lambda b, c, pt: (b, 0, 0)),
        scratch_shapes=[pltpu.VMEM((rows, 1), F32), pltpu.VMEM((rows, 1), F32), pltpu.VMEM((rows, HD_ATT), F32),
                        pltpu.VMEM((1, flat), F32), pltpu.VMEM((np_, flat), F32),
                        pltpu.VMEM((np_ * flat, HD_ATT), BF16), pltpu.VMEM((np_ * flat, HD_ATT), BF16)])
    return pl.pallas_call(
        functools.partial(_fox_sample_body, n_pages_step=np_, n_new=n_new),
        grid_spec=grid_spec, out_shape=jax.ShapeDtypeStruct((bsz, n_new, W_ATT), BF16),
        compiler_params=_cparams(("parallel", "arbitrary"), 48), name="fox_sample")(
            page_table, q_rows, *([cache_k] * np_), *([cache_v] * np_), *([cache_f] * np_), k_new, v_new, f_new)


def _convb_body(*refs, tm, has_prev):
    if has_prev:
        u_ref, prev_ref, hist0_ref, w_ref, b_ref, g_ref, beta_ref, o_ref, ext_ref = refs
        hist = jnp.where(pl.program_id(1) == 0, hist0_ref[...], prev_ref[...])
    else:
        u_ref, hist0_ref, w_ref, b_ref, g_ref, beta_ref, o_ref, ext_ref = refs
        hist = hist0_ref[...]
    halo = CONVB_HALO
    ext_ref[0:halo, :] = hist
    ext_ref[halo:halo + tm, :] = u_ref[...]
    lead = halo - (CONV_WIDTH - 1)
    ch = min(CONV_CHUNK, tm)
    for r0 in range(0, tm, ch):
        acc = jnp.broadcast_to(b_ref[...], (ch, b_ref.shape[1]))
        for k in range(CONV_WIDTH):
            acc = acc + w_ref[k:k + 1, :] * ext_ref[r0 + k + lead:r0 + k + lead + ch, :]
        y = _layer_norm(acc, g_ref[...], beta_ref[...])
        o_ref[r0:r0 + ch, :] = (y * jax.nn.sigmoid(y)).astype(o_ref.dtype)


def _convb(u, hist0, w, b, g, beta):
    bsz, t, c = u.shape
    tm = min(SEQ_TILE, t)
    nt = t // tm
    has_prev = nt > 1
    halo = CONVB_HALO
    r = tm // halo if has_prev else 1
    in_specs = [pl.BlockSpec((None, tm, c), lambda bi, i: (bi, i, 0))]
    args = [u]
    if has_prev:
        in_specs.append(pl.BlockSpec((None, halo, c), lambda bi, i: (bi, jnp.maximum(i * r - 1, 0), 0)))
        args.append(u)
    in_specs += [pl.BlockSpec((None, halo, c), lambda bi, i: (bi, 0, 0)),
                 _const_spec(w.shape), _const_spec(b.shape), _const_spec(g.shape), _const_spec(beta.shape)]
    args += [hist0, w, b, g, beta]
    return pl.pallas_call(
        functools.partial(_convb_body, tm=tm, has_prev=has_prev), grid=(bsz, nt),
        in_specs=in_specs, out_specs=pl.BlockSpec((None, tm, c), lambda bi, i: (bi, i, 0)),
        out_shape=jax.ShapeDtypeStruct((bsz, t, c), BF16),
        scratch_shapes=[pltpu.VMEM((halo + tm, c), F32)],
        compiler_params=_cparams(("parallel", "arbitrary"), 24), name="conformer_conv")(*args)


def _rglru_body(*refs, tm, has_prev):
    if has_prev:
        (x_ref, prev_ref, hist0_ref, gg_ref, h0_ref, w_ref, b_ref, wa_ref, ba_ref, wx_ref, bx_ref, lam_ref,
         o_ref, hl_ref, ext_ref, h_ref) = refs
        hist = jnp.where(pl.program_id(1) == 0, hist0_ref[...], prev_ref[...])
    else:
        (x_ref, hist0_ref, gg_ref, h0_ref, w_ref, b_ref, wa_ref, ba_ref, wx_ref, bx_ref, lam_ref,
         o_ref, hl_ref, ext_ref, h_ref) = refs
        hist = hist0_ref[...]

    @pl.when(pl.program_id(1) == 0)
    def _():
        h_ref[...] = jnp.broadcast_to(h0_ref[...], h_ref.shape)

    halo = CONVC_HALO
    ext_ref[0:halo, :] = hist
    ext_ref[halo:halo + tm, :] = x_ref[...]
    lead = halo - (RG_CONV_WIDTH - 1)
    xc = jnp.broadcast_to(b_ref[...], x_ref.shape)
    for k in range(RG_CONV_WIDTH):
        xc = xc + w_ref[k:k + 1, :] * ext_ref[k + lead:k + lead + tm, :]
    xcb = xc.astype(BF16)
    r = jax.nn.sigmoid(_dot(xcb, wa_ref[...]) + ba_ref[...])
    ig = jax.nn.sigmoid(_dot(xcb, wx_ref[...]) + bx_ref[...])
    log_a = -RG_C * r * _softplus(-lam_ref[...])
    a = jnp.exp(log_a)
    u = jnp.sqrt(-jnp.tanh(log_a) * (1.0 + a * a)) * (ig * xc)
    row = lax.broadcasted_iota(jnp.int32, a.shape, 0)
    d = 1
    while d < tm:
        a_sh = pltpu.roll(a, d, axis=0)
        u_sh = pltpu.roll(u, d, axis=0)
        keep = row >= d
        u = jnp.where(keep, a * u_sh + u, u)
        a = jnp.where(keep, a * a_sh, a)
        d *= 2
    h = a * h_ref[0:1, :] + u
    o_ref[...] = (h * gg_ref[...]).astype(o_ref.dtype)
    last = h[tm - 1:tm, :]
    h_ref[...] = jnp.broadcast_to(last, h_ref.shape)
    hl_ref[...] = last


def _rglru(x, gg, hist0, h0, w, b, wa, ba, wx, bx, lam):
    bsz, t, c = x.shape
    tm = min(SEQ_TILE, t)
    nt = t // tm
    has_prev = nt > 1
    halo = CONVC_HALO
    r = tm // halo if has_prev else 1
    tile = pl.BlockSpec((None, tm, c), lambda bi, i: (bi, i, 0))
    in_specs = [tile]
    args = [x]
    if has_prev:
        in_specs.append(pl.BlockSpec((None, halo, c), lambda bi, i: (bi, jnp.maximum(i * r - 1, 0), 0)))
        args.append(x)
    in_specs += [pl.BlockSpec((None, halo, c), lambda bi, i: (bi, 0, 0)), tile,
                 pl.BlockSpec((None, 1, c), lambda bi, i: (bi, 0, 0))]
    args += [hist0, gg, h0]
    consts = [w, b, wa, ba, wx, bx, lam]
    in_specs += [_const_spec(a.shape) for a in consts]
    args += consts
    return pl.pallas_call(
        functools.partial(_rglru_body, tm=tm, has_prev=has_prev), grid=(bsz, nt),
        in_specs=in_specs,
        out_specs=(tile, pl.BlockSpec((None, 1, c), lambda bi, i: (bi, 0, 0))),
        out_shape=(jax.ShapeDtypeStruct((bsz, t, c), BF16), jax.ShapeDtypeStruct((bsz, 1, c), F32)),
        scratch_shapes=[pltpu.VMEM((halo + tm, c), F32), pltpu.VMEM((SUBLANES, c), F32)],
        compiler_params=_cparams(("parallel", "arbitrary"), 24), name="rglru")(*args)


def _merge_body(x_ref, oa_ref, ub_ref, hc_ref, gates_ref, wa_ref, wb_ref, bb_ref, wc_ref, wo_ref, g_ref, b_ref,
                wr_ref, br_ref, x1p_ref, idx_ref, tw_ref, *, alpha):
    d = x_ref.shape[1]
    ya = _dot(oa_ref[...], wa_ref[...])
    yb = _dot(ub_ref[...], wb_ref[...]) + bb_ref[...]
    yc = _dot(hc_ref[...], wc_ref[...])
    merged = gates_ref[:, 0:d] * ya + gates_ref[:, d:2 * d] * yb + gates_ref[:, 2 * d:3 * d] * yc
    x1 = _layer_norm(alpha * x_ref[...] + _dot(merged.astype(BF16), wo_ref[...]), g_ref[...], b_ref[...])
    for p in range(x1p_ref.shape[0]):
        x1p_ref[p] = x1[:, p * PLANE_WIDTH:(p + 1) * PLANE_WIDTH]
    logits = jnp.dot(x1, wr_ref[...], preferred_element_type=F32, precision=lax.Precision.HIGHEST) + br_ref[...]
    lane = lax.broadcasted_iota(jnp.int32, logits.shape, 1)
    vals = logits
    idx_out = jnp.zeros(logits.shape, jnp.int32)
    val_out = jnp.full(logits.shape, NEG_BIG, F32)
    for k in range(TOP_K):
        m = jnp.max(vals, axis=-1, keepdims=True)
        idx = jnp.min(jnp.where(vals == m, lane, LANES), axis=-1, keepdims=True)
        idx_out = jnp.where(lane == k, idx, idx_out)
        val_out = jnp.where(lane == k, m, val_out)
        vals = jnp.where(lane == idx, -jnp.inf, vals)
    e = jnp.exp(val_out - jnp.max(val_out, axis=-1, keepdims=True))
    e = jnp.where(lane < TOP_K, e, 0.0)
    idx_ref[...] = idx_out[:, :TOP_K]
    tw_ref[...] = (e / jnp.sum(e, axis=-1, keepdims=True))[:, :TOP_K]


def _merge(x, oa, ub, hc, gates, wts, alpha):
    n, d = x.shape
    tm = TOKEN_TILE

    def row(c):
        return pl.BlockSpec((tm, c), lambda i: (i, 0))

    return pl.pallas_call(
        functools.partial(_merge_body, alpha=alpha), grid=(n // tm,),
        in_specs=[row(d), row(oa.shape[1]), row(ub.shape[1]), row(hc.shape[1]), row(gates.shape[1])]
        + [_const_spec(a.shape) for a in wts],
        out_specs=(pl.BlockSpec((d // PLANE_WIDTH, tm, PLANE_WIDTH), lambda i: (0, i, 0)), row(TOP_K), row(TOP_K)),
        out_shape=(jax.ShapeDtypeStruct((d // PLANE_WIDTH, n, PLANE_WIDTH), F32),
                   jax.ShapeDtypeStruct((n, TOP_K), jnp.int32), jax.ShapeDtypeStruct((n, TOP_K), F32)),
        compiler_params=_cparams(("parallel",), 40), name="merge_router")(x, oa, ub, hc, gates, *wts)


def _moe_body(te_ref, tf_ref, tr_ref, xs_ref, wgu_ref, bgu_ref, wd_ref, bd_ref, ys_ref, wgub_ref, wdb_ref):
    del te_ref
    i = pl.program_id(0)
    n_planes, tm, pw = xs_ref.shape

    @pl.when(tf_ref[i] == 1)
    def _():
        wgub_ref[...] = wgu_ref[...].astype(BF16)
        wdb_ref[...] = wd_ref[...].astype(BF16)

    @pl.when(tr_ref[i] > 0)
    def _():
        dff = wd_ref.shape[0]
        live = lax.broadcasted_iota(jnp.int32, (tm, pw), 0) < tr_ref[i]
        gu = bgu_ref[...]
        for p in range(n_planes):
            xp = jnp.where(live, xs_ref[p], 0.0).astype(BF16)
            gu = gu + _dot(xp, wgub_ref[p * pw:(p + 1) * pw, :])
        gate = jnp.minimum(gu[:, :dff], SWIGLU_LIMIT)
        up = jnp.clip(gu[:, dff:], -SWIGLU_LIMIT, SWIGLU_LIMIT)
        hid = (up + 1.0) * gate * jax.nn.sigmoid(SWIGLU_ALPHA * gate)
        out = _dot(hid.astype(BF16), wdb_ref[...]) + bd_ref[...]
        for p in range(n_planes):
            ys_ref[p] = out[:, p * pw:(p + 1) * pw]

    @pl.when(tr_ref[i] == 0)
    def _():
        ys_ref[...] = jnp.zeros_like(ys_ref)


def _moe_experts(layer, tile_expert, tile_first, tile_rows, xs, w_gate_up, b_gate_up, w_down, b_down):
    n_planes, r, pw = xs.shape
    d = n_planes * pw
    tm = MOE_TILE
    n_exp, _, dgu = w_gate_up.shape[1:]
    dff = w_down.shape[2]
    bgu = b_gate_up.reshape(-1, 1, dgu)
    bd = b_down.reshape(-1, 1, d)
    grid_spec = pltpu.PrefetchScalarGridSpec(
        num_scalar_prefetch=3, grid=(r // tm,),
        in_specs=[pl.BlockSpec((n_planes, tm, pw), lambda i, te, tf, tr: (0, i, 0)),
                  pl.BlockSpec((None, None, d, dgu), lambda i, te, tf, tr: (layer, te[i], 0, 0)),
                  pl.BlockSpec((None, 1, dgu), lambda i, te, tf, tr: (layer * n_exp + te[i], 0, 0)),
                  pl.BlockSpec((None, None, dff, d), lambda i, te, tf, tr: (layer, te[i], 0, 0)),
                  pl.BlockSpec((None, 1, d), lambda i, te, tf, tr: (layer * n_exp + te[i], 0, 0))],
        out_specs=pl.BlockSpec((n_planes, tm, pw), lambda i, te, tf, tr: (0, i, 0)),
        scratch_shapes=[pltpu.VMEM((d, dgu), BF16), pltpu.VMEM((dff, d), BF16)])
    return pl.pallas_call(
        _moe_body, grid_spec=grid_spec, out_shape=jax.ShapeDtypeStruct((n_planes, r, pw), F32),
        compiler_params=_cparams(("arbitrary",), 56), name="moe_experts")(
            tile_expert, tile_first, tile_rows, xs, w_gate_up, bgu, w_down, bd)


def _combine_body(x1p_ref, yg_ref, tw_ref, g_ref, b_ref, o_ref, *, alpha):
    tw = tw_ref[...]
    cols = []
    for p in range(x1p_ref.shape[0]):
        acc = alpha * x1p_ref[p]
        for k in range(TOP_K):
            acc = acc + tw[:, k:k + 1] * yg_ref[k, p]
        cols.append(acc)
    o_ref[...] = _layer_norm(jnp.concatenate(cols, axis=1), g_ref[...], b_ref[...])


def _combine(x1p, yg, tw, g, b, alpha):
    n_planes, n, pw = x1p.shape
    d = n_planes * pw
    tm = TOKEN_TILE
    return pl.pallas_call(
        functools.partial(_combine_body, alpha=alpha), grid=(n // tm,),
        in_specs=[pl.BlockSpec((n_planes, tm, pw), lambda i: (0, i, 0)),
                  pl.BlockSpec((TOP_K, n_planes, tm, pw), lambda i: (0, 0, i, 0)),
                  pl.BlockSpec((tm, TOP_K), lambda i: (i, 0)), _const_spec(g.shape), _const_spec(b.shape)],
        out_specs=pl.BlockSpec((tm, d), lambda i: (i, 0)), out_shape=jax.ShapeDtypeStruct((n, d), F32),
        compiler_params=_cparams(("parallel",), 32), name="moe_combine_norm")(x1p, yg, tw, g, b)


def _dispatch_body(idx_ref, pos_ref, meta_ref, cnt_ref, base_ref):
    ph = pl.program_id(0)
    i = pl.program_id(1)
    tb = idx_ref.shape[0]
    lane = lax.broadcasted_iota(jnp.int32, (tb, LANES), 1)
    idx = idx_ref[...]
    onehot = [(idx[:, k:k + 1] == lane).astype(F32) for k in range(TOP_K)]
    colsum = [jnp.sum(o, axis=0, keepdims=True) for o in onehot]

    @pl.when(jnp.logical_and(ph == 0, i == 0))
    def _():
        cnt_ref[...] = jnp.zeros_like(cnt_ref)

    @pl.when(ph == 0)
    def _():
        cnt_ref[...] = cnt_ref[...] + (colsum[0] + colsum[1] + colsum[2] + colsum[3])

    @pl.when(jnp.logical_and(ph == 1, i == 0))
    def _():
        cnt = cnt_ref[...]
        ptiles = jnp.floor((cnt + (MOE_TILE - 1)) * (1.0 / MOE_TILE))
        tile_end = _lane_cumsum(ptiles)
        base_ref[...] = (tile_end - ptiles) * MOE_TILE
        row = lax.broadcasted_iota(jnp.int32, meta_ref.shape, 0)
        meta_ref[...] = jnp.where(row == 0, tile_end[0:1, :], jnp.where(row == 1, cnt[0:1, :], 0.0)).astype(jnp.int32)

    @pl.when(ph == 1)
    def _():
        earlier = (lax.broadcasted_iota(jnp.int32, (tb, tb), 0) > lax.broadcasted_iota(jnp.int32, (tb, tb), 1))
        earlier = earlier.astype(BF16)
        base = base_ref[0:1, :]
        slot = lax.broadcasted_iota(jnp.int32, (tb, TOP_K), 1)
        pos = jnp.zeros((tb, TOP_K), F32)
        for k in range(TOP_K):
            seen = _dot(earlier, onehot[k].astype(BF16)) + base
            pos = jnp.where(slot == k, jnp.sum(onehot[k] * seen, axis=1, keepdims=True), pos)
            base = base + colsum[k]
        base_ref[...] = jnp.broadcast_to(base, base_ref.shape)
        pos_ref[...] = pos.astype(jnp.int32)


def _dispatch(top_idx):
    n = top_idx.shape[0]
    tm = MOE_TILE
    n_tiles = -(-(n * TOP_K) // tm) + N_EXPERTS
    tb = next(t for t in DISPATCH_TILES if n % t == 0)
    pos, meta = pl.pallas_call(
        _dispatch_body, grid=(2, n // tb),
        in_specs=[pl.BlockSpec((tb, TOP_K), lambda ph, i: (i, 0))],
        out_specs=(pl.BlockSpec((tb, TOP_K), lambda ph, i: (i * ph, 0)), _const_spec((SUBLANES, LANES))),
        out_shape=(jax.ShapeDtypeStruct((n, TOP_K), jnp.int32), jax.ShapeDtypeStruct((SUBLANES, LANES), jnp.int32)),
        scratch_shapes=[pltpu.VMEM((SUBLANES, LANES), F32), pltpu.VMEM((SUBLANES, LANES), F32)],
        compiler_params=_cparams(("arbitrary", "arbitrary"), 32), name="moe_dispatch")(top_idx)
    tile_end, cnt = meta[0, :N_EXPERTS], meta[1, :N_EXPERTS]
    tiles = jnp.arange(n_tiles, dtype=jnp.int32)
    te = jnp.minimum(jnp.sum((tiles[:, None] >= tile_end[None, :]).astype(jnp.int32), axis=1), N_EXPERTS - 1)
    tile_start = tile_end - (cnt + tm - 1) // tm
    tr = jnp.clip(cnt[te] - (tiles - tile_start[te]) * tm, 0, tm) * (tiles < tile_end[-1]).astype(jnp.int32)
    tf = jnp.concatenate([jnp.ones((1,), jnp.int32), (te[1:] != te[:-1]).astype(jnp.int32)])
    return pos, te, tf, tr, n_tiles * tm


def _sc_mesh():
    return plsc.VectorSubcoreMesh(core_axis_name="c", subcore_axis_name="s")


def _sc_gather_rows(table, idx):
    m = idx.shape[0]
    w = table.shape[1]

    @functools.partial(pl.kernel, out_type=jax.ShapeDtypeStruct((m, w), table.dtype), mesh=_sc_mesh())
    def gather(t_hbm, i_hbm, o_hbm):
        def body(i_vmem, o_vmem):
            pltpu.sync_copy(t_hbm.at[i_vmem.at[0]], o_vmem)

        pltpu.emit_pipeline(
            body, grid=(m // SC_WINDOW,),
            in_specs=[pl.BlockSpec((1, SC_WINDOW), index_map=lambda i: (0, i))],
            out_specs=[pl.BlockSpec((SC_WINDOW, w), index_map=lambda i: (i, 0))],
            core_axis_name=("c", "s"), dimension_semantics=(pltpu.PARALLEL,))(i_hbm, o_hbm)

    return gather(table, idx.reshape(1, m))


def _sc_scatter_rows(rows, idx, n_out):
    n_src, w = rows.shape
    m = idx.shape[0]
    src_blocks = n_src // SC_WINDOW

    @functools.partial(pl.kernel, out_type=jax.ShapeDtypeStruct((n_out, w), rows.dtype), mesh=_sc_mesh(),
                       scratch_types=[])
    def scatter(x_hbm, i_hbm, o_hbm):
        def body(x_vmem, i_vmem):
            pltpu.sync_copy(x_vmem, o_hbm.at[i_vmem.at[0]])

        pltpu.emit_pipeline(
            body, grid=(m // SC_WINDOW,),
            in_specs=[pl.BlockSpec((SC_WINDOW, w), index_map=lambda i: (lax.rem(i, src_blocks), 0)),
                      pl.BlockSpec((1, SC_WINDOW), index_map=lambda i: (0, i))],
            out_specs=[], core_axis_name=("c", "s"), dimension_semantics=(pltpu.PARALLEL,))(x_hbm, i_hbm)

    return scatter(rows, idx.reshape(1, m))


def _plane_rows(pos, n_planes, rows_per_plane):
    off = jnp.arange(n_planes, dtype=jnp.int32) * rows_per_plane
    return (pos.T[:, None, :] + off[None, :, None]).reshape(-1)


def _pad_rows(a, rows):
    return jnp.pad(a, ((0, 0), (rows - a.shape[1], 0), (0, 0)))


def _block_diag(w):
    h, bd, _ = w.shape
    eye = jnp.eye(h, dtype=w.dtype)
    return jnp.einsum("hij,hg->higj", w, eye).reshape(h * bd, h * bd)


def kernel(x_prompt, x_sample, cache_k, cache_v, cache_logf, state_conv_b, state_conv_c, state_h, page_table,
           w_in, b_f, w_att_o, conv_b_w, conv_b_b, lnb_g, lnb_b, w_conv_o, b_conv_o, conv_c_w, conv_c_b,
           w_rg_a, b_rg_a, w_rg_x, b_rg_x, rg_lambda, w_rg_o, w_out, ln1_g, ln1_b, w_router, b_router,
           w_gate_up, b_gate_up, w_down, b_down, ln2_g, ln2_b):
    bp, seq, d = x_prompt.shape
    bs, ns, _ = x_sample.shape
    depth, n_pool, page = cache_k.shape[:3]
    w_conv = state_conv_b.shape[-1]
    w_rg = state_conv_c.shape[-1]
    alpha = (2 * depth) ** 0.25
    n_p, n_s = bp * seq, bs * ns
    off_f = 3 * W_ATT
    off_glu = off_f + H_ATT
    off_rg = off_glu + 2 * w_conv
    off_gate = off_rg + 2 * w_rg

    cf = cache_logf.reshape(depth * n_pool, 1, page * H_ATT)

    xp = x_prompt.reshape(n_p, d)
    xs = x_sample.reshape(n_s, d)
    outs = {k: [] for k in ("kp", "vp", "fp", "ks", "vs", "fs", "cbp", "cbs", "ccp", "ccs", "hp", "hs")}

    for l in range(depth):
        wl = w_in[l]
        wf = wl[:, off_f:off_glu]
        in_w = (wl[:, :off_f].astype(BF16), wl[:, :W_ATT].T.astype(BF16), wl[:, 2 * W_ATT:off_f].T.astype(BF16),
                jnp.pad(wf, ((0, 0), (0, LANES - H_ATT))).astype(BF16), b_f[l].reshape(1, H_ATT),
                wl[:, off_glu:off_rg].astype(BF16), wl[:, off_rg:off_gate].astype(BF16),
                wl[:, off_gate:].astype(BF16))
        cbw = jnp.pad(conv_b_w[l], ((0, CONVB_HALO - CONV_WIDTH), (0, 0)))
        ccw = jnp.pad(conv_c_w[l], ((0, CONVC_HALO - RG_CONV_WIDTH), (0, 0)))
        rg_w = (ccw, conv_c_b[l].reshape(1, -1), _block_diag(w_rg_a[l]).astype(BF16), b_rg_a[l].reshape(1, -1),
                _block_diag(w_rg_x[l]).astype(BF16), b_rg_x[l].reshape(1, -1), rg_lambda[l].reshape(1, -1))
        cb_w = (cbw, conv_b_b[l].reshape(1, -1), lnb_g[l].reshape(1, -1), lnb_b[l].reshape(1, -1))
        wr = jnp.pad(w_router[l], ((0, 0), (0, LANES - N_EXPERTS)))
        br = jnp.pad(b_router[l], (0, LANES - N_EXPERTS), constant_values=NEG_BIG).reshape(1, LANES)
        mg_w = (w_att_o[l].astype(BF16), w_conv_o[l].astype(BF16), b_conv_o[l].reshape(1, -1),
                w_rg_o[l].astype(BF16), w_out[l].astype(BF16), ln1_g[l].reshape(1, -1), ln1_b[l].reshape(1, -1),
                wr, br)

        (q, k, v, kb, qtm, vt, lf, u, rgx, gg, gates) = _inproj(xp, in_w)
        oa = _fox_prompt(qtm, kb, vt, _cumsum(lf, seq), bp, seq)
        u3 = u.reshape(bp, seq, w_conv)
        ub = _convb(u3, jnp.zeros((bp, CONVB_HALO, w_conv), F32), *cb_w)
        rgx3 = rgx.reshape(bp, seq, w_rg)
        hc, hl = _rglru(rgx3, gg.reshape(bp, seq, w_rg), jnp.zeros((bp, CONVC_HALO, w_rg), F32),
                        jnp.zeros((bp, 1, w_rg), F32), *rg_w)
        x1p, tip, twp = _merge(xp, oa, ub.reshape(n_p, w_conv), hc.reshape(n_p, w_rg), gates, mg_w, alpha)
        outs["kp"].append(k.reshape(bp, seq, H_ATT, HD_ATT))
        outs["vp"].append(v.reshape(bp, seq, H_ATT, HD_ATT))
        outs["fp"].append(lf.reshape(bp, seq, H_ATT))
        outs["cbp"].append(u3[:, seq - (CONV_WIDTH - 1):])
        outs["ccp"].append(rgx3[:, seq - (RG_CONV_WIDTH - 1):])
        outs["hp"].append(hl.reshape(bp, w_rg))

        (q, k, v, kb, qtm, vt, lf, u, rgx, gg, gates) = _inproj(xs, in_w)
        q_rows = q.reshape(bs, ns, H_ATT, HD_ATT).transpose(0, 2, 1, 3).reshape(bs, H_ATT * ns, HD_ATT)
        pad_new = ((0, 0), (0, page - ns), (0, 0), (0, 0))
        oa = _fox_sample(l, page_table, q_rows, cache_k, cache_v, cf,
                         jnp.pad(k.reshape(bs, ns, H_ATT, HD_ATT), pad_new),
                         jnp.pad(v.reshape(bs, ns, H_ATT, HD_ATT), pad_new),
                         jnp.pad(lf.reshape(bs, 1, ns * H_ATT), ((0, 0), (0, 0), (0, (page - ns) * H_ATT))), ns)
        u3 = u.reshape(bs, ns, w_conv)
        ub = _convb(u3, _pad_rows(state_conv_b[l], CONVB_HALO), *cb_w)
        rgx3 = rgx.reshape(bs, ns, w_rg)
        hc, hl = _rglru(rgx3, gg.reshape(bs, ns, w_rg), _pad_rows(state_conv_c[l], CONVC_HALO),
                        state_h[l].reshape(bs, 1, w_rg), *rg_w)
        x1s, tis, tws = _merge(xs, oa.reshape(n_s, W_ATT), ub.reshape(n_s, w_conv), hc.reshape(n_s, w_rg),
                                     gates, mg_w, alpha)
        outs["ks"].append(k.reshape(bs, ns, H_ATT, HD_ATT))
        outs["vs"].append(v.reshape(bs, ns, H_ATT, HD_ATT))
        outs["fs"].append(lf.reshape(bs, ns, H_ATT))
        outs["cbs"].append(jnp.concatenate([state_conv_b[l], u3], axis=1)[:, ns:])
        outs["ccs"].append(jnp.concatenate([state_conv_c[l], rgx3], axis=1)[:, ns:])
        outs["hs"].append(hl.reshape(bs, w_rg))

        pos, te, tf, tr, n_rows = _dispatch(jnp.concatenate([tip, tis], axis=0))
        x1_all = jnp.concatenate([x1p, x1s], axis=1)
        n_planes, n_all, pw = x1_all.shape
        xs_rows = _sc_scatter_rows(x1_all.reshape(n_planes * n_all, pw), _plane_rows(pos, n_planes, n_rows),
                                   n_planes * n_rows)
        ys = _moe_experts(l, te, tf, tr, xs_rows.reshape(n_planes, n_rows, pw), w_gate_up, b_gate_up, w_down, b_down)
        ys = ys.reshape(n_planes * n_rows, pw)
        g2, b2 = ln2_g[l].reshape(1, -1), ln2_b[l].reshape(1, -1)
        yg = _sc_gather_rows(ys, _plane_rows(pos[:n_p], n_planes, n_rows)).reshape(TOP_K, n_planes, n_p, pw)
        xp = _combine(x1p, yg, twp, g2, b2, alpha)
        yg = _sc_gather_rows(ys, _plane_rows(pos[n_p:], n_planes, n_rows)).reshape(TOP_K, n_planes, n_s, pw)
        xs = _combine(x1s, yg, tws, g2, b2, alpha)

    st = {k: jnp.stack(v, 0) for k, v in outs.items()}
    return (xp.reshape(bp, seq, d), xs.reshape(bs, ns, d), st["kp"], st["vp"], st["fp"], st["ks"], st["vs"],
            st["fs"], st["cbp"], st["cbs"], st["ccp"], st["ccs"], st["hp"], st["hs"])
```

```python
import functools
import math

import jax
import jax.numpy as jnp
from jax import lax
from jax.experimental import pallas as pl
from jax.experimental.pallas import tpu as pltpu
from jax.experimental.pallas import tpu_sc as plsc

F32 = jnp.float32
BF16 = jnp.bfloat16

H_ATT = 8
HD_ATT = 64
W_ATT = H_ATT * HD_ATT
CONV_WIDTH = 31
RG_CONV_WIDTH = 4
H_RG = 8
RG_C = 8.0
N_EXPERTS = 32
TOP_K = 4
SWIGLU_LIMIT = 7.0
SWIGLU_ALPHA = 1.702
LN_EPS = 1e-5
N_BRANCH = 3

LANES = 128
SUBLANES = 8
NEG_BIG = -1e30

TOKEN_TILE = 256
MERGE_TILE = 512
ATT_TILE = 512
CUMSUM_TILE = 512
SEQ_TILE = 256
CONVB_HALO = 32
CONVC_HALO = 8
CONV_CHUNK = 32
PAGES_PER_STEP = 8
MOE_TILE = 256
PLANE_WIDTH = 256
SC_WINDOW = 128
DISPATCH_TILES = (1280, 1024, 768, 512, 256)


def _cparams(sem, vmem_mb):
    return pltpu.CompilerParams(dimension_semantics=sem, vmem_limit_bytes=vmem_mb << 20)


def _const_spec(shape):
    nd = len(shape)
    return pl.BlockSpec(shape, lambda *_: (0,) * nd)


def _log_sigmoid(z):
    return jnp.minimum(z, 0.0) - jnp.log1p(jnp.exp(-jnp.abs(z)))


def _softplus(z):
    return jnp.maximum(z, 0.0) + jnp.log1p(jnp.exp(-jnp.abs(z)))


def _layer_norm(x, g, b):
    mu = jnp.mean(x, axis=-1, keepdims=True)
    xc = x - mu
    var = jnp.mean(xc * xc, axis=-1, keepdims=True)
    return xc * lax.rsqrt(var + LN_EPS) * g + b


def _dot(a, b):
    return jnp.dot(a, b, preferred_element_type=F32)


def _dot_nt(a, b, precision=None):
    return lax.dot_general(a, b, (((1,), (1,)), ((), ())), preferred_element_type=F32, precision=precision)


def _inproj_body(x_ref, wqk_ref, wt_ref, wf_ref, wft_ref, bf_ref, bft_ref, wglu_ref, wrg_ref, wgate_ref,
                 q_ref, kb_ref, qtm_ref, kt_ref, vt_ref, vtb_ref, lf_ref, lft_ref, u_ref, rgx_ref, gg_ref, gates_ref):
    xb = x_ref[...].astype(BF16)
    w = W_ATT
    scale = HD_ATT ** -0.5
    qk = _dot(xb, wqk_ref[...])
    q_ref[...] = (qk[:, :w] * scale).astype(BF16)
    kb_ref[...] = qk[:, w:].astype(BF16)
    t = _dot_nt(wt_ref[...], xb)
    kt_ref[...] = t[w:2 * w, :]
    vt = t[2 * w:, :]
    vt_ref[...] = vt
    vtb_ref[...] = vt.astype(BF16)
    qt = (t[:w, :] * scale).astype(BF16)
    pair = 2 * HD_ATT
    lower = lax.broadcasted_iota(jnp.int32, (pair, qt.shape[1]), 0) < HD_ATT
    for h in range(H_ATT):
        slab = qt[(h // 2) * pair:(h // 2 + 1) * pair, :]
        qtm_ref[h] = jnp.where(lower if h % 2 == 0 else jnp.logical_not(lower), slab, jnp.zeros_like(slab))
    zf = _dot(xb, wf_ref[...])
    lf_ref[...] = _log_sigmoid(zf[:, :H_ATT] + bf_ref[...])
    lft_ref[...] = _log_sigmoid(_dot_nt(wft_ref[...], xb) + bft_ref[...])
    glu = _dot(xb, wglu_ref[...])
    c = glu.shape[1] // 2
    u_ref[...] = glu[:, :c] * jax.nn.sigmoid(glu[:, c:])
    rg = _dot(xb, wrg_ref[...])
    c = rg.shape[1] // 2
    rgx_ref[...] = rg[:, :c]
    gg_ref[...] = jax.nn.gelu(rg[:, c:])
    d = gates_ref.shape[1] // N_BRANCH
    for j in range(N_BRANCH):
        gates_ref[:, j * d:(j + 1) * d] = jax.nn.sigmoid(_dot(xb, wgate_ref[:, j * d:(j + 1) * d]))


def _inproj(x, wts, seq):
    n, d = x.shape
    tm = TOKEN_TILE
    wglu, wrg, wgate = wts[-3:]
    wc, wr = wglu.shape[1] // 2, wrg.shape[1] // 2
    bps = seq // tm

    def row(c):
        return pl.BlockSpec((tm, c), lambda i: (i, 0))

    def col(r):
        return pl.BlockSpec((None, r, tm), lambda i: (i // bps, 0, i % bps))

    def colshape(r, dt):
        return jax.ShapeDtypeStruct((n // seq, r, seq), dt)

    out_shape = (
        jax.ShapeDtypeStruct((n, W_ATT), BF16), jax.ShapeDtypeStruct((n, W_ATT), BF16),
        jax.ShapeDtypeStruct((H_ATT, 2 * HD_ATT, n), BF16),
        colshape(W_ATT, F32), colshape(W_ATT, F32), colshape(W_ATT, BF16),
        jax.ShapeDtypeStruct((n, H_ATT), F32), colshape(H_ATT, F32), jax.ShapeDtypeStruct((n, wc), F32),
        jax.ShapeDtypeStruct((n, wr), F32), jax.ShapeDtypeStruct((n, wr), F32),
        jax.ShapeDtypeStruct((n, wgate.shape[1]), F32))
    out_specs = (row(W_ATT), row(W_ATT), pl.BlockSpec((H_ATT, 2 * HD_ATT, tm), lambda i: (0, 0, i)),
                 col(W_ATT), col(W_ATT), col(W_ATT),
                 row(H_ATT), col(H_ATT), row(wc), row(wr), row(wr), row(wgate.shape[1]))
    return pl.pallas_call(
        _inproj_body, grid=(n // tm,),
        in_specs=[row(d)] + [_const_spec(a.shape) for a in wts],
        out_specs=out_specs, out_shape=out_shape,
        compiler_params=_cparams(("parallel",), 56), name="inproj")(x, *wts)


def _lane_cumsum(c):
    n = c.shape[1]
    lane = lax.broadcasted_iota(jnp.int32, c.shape, 1)
    d = 1
    while d < n:
        c = c + jnp.where(lane >= d, pltpu.roll(c, d, axis=1), 0.0)
        d *= 2
    return c


def _cumsum_body(f_ref, c_ref, carry_ref, *, blocks_per_seq):
    @pl.when(pl.program_id(0) % blocks_per_seq == 0)
    def _():
        carry_ref[...] = jnp.zeros_like(carry_ref)

    c = f_ref[...]
    n = c.shape[0]
    row = lax.broadcasted_iota(jnp.int32, c.shape, 0)
    d = 1
    while d < n:
        c = c + jnp.where(row >= d, pltpu.roll(c, d, axis=0), 0.0)
        d *= 2
    c = c + carry_ref[0:1, :]
    carry_ref[...] = jnp.broadcast_to(c[n - 1:n, :], carry_ref.shape)
    for h in range(H_ATT):
        c_ref[h] = jnp.broadcast_to(c[:, h:h + 1], (n, LANES))


def _cumsum(lf, seq):
    n, h = lf.shape
    tc = min(CUMSUM_TILE, seq)
    return pl.pallas_call(
        functools.partial(_cumsum_body, blocks_per_seq=seq // tc), grid=(n // tc,),
        in_specs=[pl.BlockSpec((tc, h), lambda i: (i, 0))],
        out_specs=pl.BlockSpec((h, tc, LANES), lambda i: (0, i, 0)),
        out_shape=jax.ShapeDtypeStruct((h, n, LANES), F32),
        scratch_shapes=[pltpu.VMEM((SUBLANES, h), F32)],
        compiler_params=_cparams(("arbitrary",), 16), name="logf_cumsum")(lf)


def _fox_prompt_body(qtm_ref, k_ref, vt_ref, ck_ref, cq_ref, o_ref, m_ref, l_ref, acc_ref):
    qi = pl.program_id(1)
    ki = pl.program_id(2)
    tk, tq = k_ref.shape[0], qtm_ref.shape[2]
    pair = 2 * HD_ATT

    @pl.when(ki == 0)
    def _():
        m_ref[...] = jnp.full_like(m_ref, NEG_BIG)
        l_ref[...] = jnp.zeros_like(l_ref)
        acc_ref[...] = jnp.zeros_like(acc_ref)

    def update(diagonal):
        if diagonal:
            keep = lax.broadcasted_iota(jnp.int32, (tk, tq), 0) <= lax.broadcasted_iota(jnp.int32, (tk, tq), 1)
        for h in range(H_ATT):
            s = _dot(k_ref[:, (h // 2) * pair:(h // 2 + 1) * pair], qtm_ref[h])
            bias = ck_ref[h] - cq_ref[h, 0:1, :]
            s = s - jnp.concatenate([bias] * (tq // LANES), axis=1)
            if diagonal:
                s = jnp.where(keep, s, NEG_BIG)
            m_old = m_ref[h]
            m_new = jnp.maximum(m_old, jnp.max(s, axis=0, keepdims=True))
            p = jnp.exp(s - m_new)
            alpha = jnp.exp(m_old - m_new)
            l_ref[h] = alpha * l_ref[h] + jnp.sum(p, axis=0, keepdims=True)
            acc_ref[h] = alpha * acc_ref[h] + _dot(vt_ref[h * HD_ATT:(h + 1) * HD_ATT, :], p.astype(BF16))
            m_ref[h] = m_new

    @pl.when(ki < qi)
    def _():
        update(False)

    @pl.when(ki == qi)
    def _():
        update(True)
        for j in range(H_ATT // 2):
            o2 = jnp.concatenate([acc_ref[2 * j] / l_ref[2 * j], acc_ref[2 * j + 1] / l_ref[2 * j + 1]], axis=0)
            o_ref[:, j * pair:(j + 1) * pair] = o2.T.astype(o_ref.dtype)


def _fox_prompt(qtm, kb, vt, cb, batch, seq):
    n = kb.shape[0]
    t = min(ATT_TILE, seq)
    nb = seq // t

    def kblk(b, qi, ki):
        return b * nb + jnp.minimum(ki, qi)

    return pl.pallas_call(
        _fox_prompt_body, grid=(batch, nb, nb),
        in_specs=[pl.BlockSpec((H_ATT, 2 * HD_ATT, t), lambda b, qi, ki: (0, 0, b * nb + qi)),
                  pl.BlockSpec((t, W_ATT), lambda b, qi, ki: (kblk(b, qi, ki), 0)),
                  pl.BlockSpec((None, W_ATT, t), lambda b, qi, ki: (b, 0, jnp.minimum(ki, qi))),
                  pl.BlockSpec((H_ATT, t, LANES), lambda b, qi, ki: (0, kblk(b, qi, ki), 0)),
                  pl.BlockSpec((H_ATT, SUBLANES, LANES), lambda b, qi, ki: (0, (b * nb + qi) * (t // SUBLANES), 0))],
        out_specs=pl.BlockSpec((t, W_ATT), lambda b, qi, ki: (b * nb + qi, 0)),
        out_shape=jax.ShapeDtypeStruct((n, W_ATT), BF16),
        scratch_shapes=[pltpu.VMEM((H_ATT, 1, t), F32), pltpu.VMEM((H_ATT, 1, t), F32),
                        pltpu.VMEM((H_ATT, HD_ATT, t), F32)],
        compiler_params=_cparams(("parallel", "parallel", "arbitrary"), 40), name="fox_prompt")(
            qtm, kb, vt, cb, cb)


def _fox_sample_body(pt_ref, qbd_ref, *refs, n_pages_step, n_new):
    del pt_ref
    np_ = n_pages_step
    k_refs = refs[:np_]
    v_refs = refs[np_:2 * np_]
    f_refs = refs[2 * np_:3 * np_]
    kn_ref, vn_ref, fn_ref, o_ref, m_ref, l_ref, acc_ref, carry_ref, kb_ref, vb_ref = refs[3 * np_:]
    c = pl.program_id(1)
    rows = qbd_ref.shape[0]
    page = kn_ref.shape[1]

    @pl.when(c == 0)
    def _():
        m_ref[...] = jnp.full_like(m_ref, NEG_BIG)
        l_ref[...] = jnp.zeros_like(l_ref)
        acc_ref[...] = jnp.zeros_like(acc_ref)
        carry_ref[...] = jnp.zeros_like(carry_ref)

    def decay_rows(ft):
        ct = _lane_cumsum(ft) + carry_ref[:, 0:1]
        carry_ref[...] = jnp.broadcast_to(ct[:, ct.shape[1] - 1:], carry_ref.shape)
        return jnp.concatenate([jnp.broadcast_to(ct[h:h + 1, :], (n_new, ct.shape[1])) for h in range(H_ATT)], axis=0)

    def update(kb, vb, cexp, keep):
        s = _dot(qbd_ref[...], kb) - cexp
        if keep is not None:
            s = jnp.where(keep, s, NEG_BIG)
        m_old = m_ref[...]
        m_new = jnp.maximum(m_old, jnp.max(s, axis=-1, keepdims=True))
        p = jnp.exp(s - m_new)
        alpha = jnp.exp(m_old - m_new)
        l_ref[...] = alpha * l_ref[...] + jnp.sum(p, axis=-1, keepdims=True)
        acc_ref[...] = alpha * acc_ref[...] + _dot_nt(p.astype(BF16), vb)
        m_ref[...] = m_new

    for j in range(np_):
        kb_ref[:, j * page:(j + 1) * page] = k_refs[j][...].astype(BF16)
        vb_ref[:, j * page:(j + 1) * page] = v_refs[j][...].astype(BF16)
    ft = jnp.concatenate([r[...] for r in f_refs], axis=1)
    update(kb_ref[...], vb_ref[...], decay_rows(ft), None)

    @pl.when(c == pl.num_programs(1) - 1)
    def _():
        tok = lax.broadcasted_iota(jnp.int32, (rows, page), 0) % n_new
        key = lax.broadcasted_iota(jnp.int32, (rows, page), 1)
        update(kn_ref[...].astype(BF16), vn_ref[...].astype(BF16), decay_rows(fn_ref[...]), key <= tok)
        acc = acc_ref[...] / l_ref[...]
        for h in range(H_ATT):
            o_ref[:, h * HD_ATT:(h + 1) * HD_ATT] = acc[h * n_new:(h + 1) * n_new,
                                                        h * HD_ATT:(h + 1) * HD_ATT].astype(o_ref.dtype)


def _fox_sample(layer, page_table, qbd, cache_kt, cache_vt, cache_ft, kt_new, vt_new, ft_new, n_new):
    bsz, n_pages = page_table.shape
    page = cache_kt.shape[3]
    np_ = PAGES_PER_STEP
    while n_pages % np_:
        np_ //= 2
    rows = qbd.shape[1]

    def page_spec(r, j):
        return pl.BlockSpec((None, None, r, page), lambda b, c, pt: (layer, pt[b, c * np_ + j], 0, 0))

    def seq_spec(r, w):
        return pl.BlockSpec((None, r, w), lambda b, c, pt: (b, 0, 0))

    in_specs = ([seq_spec(rows, W_ATT)]
                + [page_spec(W_ATT, j) for j in range(np_)] + [page_spec(W_ATT, j) for j in range(np_)]
                + [page_spec(H_ATT, j) for j in range(np_)]
                + [seq_spec(W_ATT, page), seq_spec(W_ATT, page), seq_spec(H_ATT, page)])
    grid_spec = pltpu.PrefetchScalarGridSpec(
        num_scalar_prefetch=1, grid=(bsz, n_pages // np_), in_specs=in_specs,
        out_specs=seq_spec(n_new, W_ATT),
        scratch_shapes=[pltpu.VMEM((rows, 1), F32), pltpu.VMEM((rows, 1), F32), pltpu.VMEM((rows, W_ATT), F32),
                        pltpu.VMEM((H_ATT, LANES), F32),
                        pltpu.VMEM((W_ATT, np_ * page), BF16), pltpu.VMEM((W_ATT, np_ * page), BF16)])
    return pl.pallas_call(
        functools.partial(_fox_sample_body, n_pages_step=np_, n_new=n_new),
        grid_spec=grid_spec, out_shape=jax.ShapeDtypeStruct((bsz, n_new, W_ATT), BF16),
        compiler_params=_cparams(("parallel", "arbitrary"), 40), name="fox_sample")(
            page_table, qbd, *([cache_kt] * np_), *([cache_vt] * np_), *([cache_ft] * np_), kt_new, vt_new, ft_new)


def _convb_body(*refs, tm, has_prev):
    if has_prev:
        u_ref, prev_ref, hist0_ref, w_ref, b_ref, g_ref, beta_ref, o_ref, ext_ref = refs
        hist = jnp.where(pl.program_id(1) == 0, hist0_ref[...], prev_ref[...])
    else:
        u_ref, hist0_ref, w_ref, b_ref, g_ref, beta_ref, o_ref, ext_ref = refs
        hist = hist0_ref[...]
    halo = CONVB_HALO
    ext_ref[0:halo, :] = hist
    ext_ref[halo:halo + tm, :] = u_ref[...]
    lead = halo - (CONV_WIDTH - 1)
    ch = min(CONV_CHUNK, tm)
    for r0 in range(0, tm, ch):
        acc = jnp.broadcast_to(b_ref[...], (ch, b_ref.shape[1]))
        for r in range(SUBLANES):
            taps = [k for k in range(CONV_WIDTH) if (k + lead) % SUBLANES == r]
            if not taps:
                continue
            win = ext_ref[r0 + r:r0 + taps[-1] + lead + ch, :]
            for k in taps:
                off = k + lead - r
                wk = jnp.concatenate([w_ref[k]] * (ch // SUBLANES), axis=0)
                acc = acc + wk * win[off:off + ch, :]
        y = _layer_norm(acc, g_ref[...], beta_ref[...])
        o_ref[r0:r0 + ch, :] = (y * jax.nn.sigmoid(y)).astype(o_ref.dtype)


def _convb(u, hist0, w, b, g, beta):
    bsz, t, c = u.shape
    tm = min(SEQ_TILE, t)
    nt = t // tm
    has_prev = nt > 1
    halo = CONVB_HALO
    r = tm // halo if has_prev else 1
    in_specs = [pl.BlockSpec((None, tm, c), lambda bi, i: (bi, i, 0))]
    args = [u]
    if has_prev:
        in_specs.append(pl.BlockSpec((None, halo, c), lambda bi, i: (bi, jnp.maximum(i * r - 1, 0), 0)))
        args.append(u)
    in_specs += [pl.BlockSpec((None, halo, c), lambda bi, i: (bi, 0, 0)),
                 _const_spec(w.shape), _const_spec(b.shape), _const_spec(g.shape), _const_spec(beta.shape)]
    args += [hist0, w, b, g, beta]
    return pl.pallas_call(
        functools.partial(_convb_body, tm=tm, has_prev=has_prev), grid=(bsz, nt),
        in_specs=in_specs, out_specs=pl.BlockSpec((None, tm, c), lambda bi, i: (bi, i, 0)),
        out_shape=jax.ShapeDtypeStruct((bsz, t, c), BF16),
        scratch_shapes=[pltpu.VMEM((halo + tm, c), F32)],
        compiler_params=_cparams(("parallel", "arbitrary"), 24), name="conformer_conv")(*args)


def _rglru_body(*refs, tm, has_prev):
    if has_prev:
        (x_ref, prev_ref, hist0_ref, gg_ref, h0_ref, w_ref, b_ref, wa_ref, ba_ref, wx_ref, bx_ref, lam_ref,
         o_ref, hl_ref, ext_ref, h_ref) = refs
        hist = jnp.where(pl.program_id(1) == 0, hist0_ref[...], prev_ref[...])
    else:
        (x_ref, hist0_ref, gg_ref, h0_ref, w_ref, b_ref, wa_ref, ba_ref, wx_ref, bx_ref, lam_ref,
         o_ref, hl_ref, ext_ref, h_ref) = refs
        hist = hist0_ref[...]

    @pl.when(pl.program_id(1) == 0)
    def _():
        h_ref[...] = jnp.broadcast_to(h0_ref[...], h_ref.shape)

    halo = CONVC_HALO
    ext_ref[0:halo, :] = hist
    ext_ref[halo:halo + tm, :] = x_ref[...]
    lead = halo - (RG_CONV_WIDTH - 1)
    xc = jnp.broadcast_to(b_ref[...], x_ref.shape)
    for k in range(RG_CONV_WIDTH):
        xc = xc + w_ref[k:k + 1, :] * ext_ref[k + lead:k + lead + tm, :]
    xcb = xc.astype(BF16)
    r = jax.nn.sigmoid(_dot(xcb, wa_ref[...]) + ba_ref[...])
    ig = jax.nn.sigmoid(_dot(xcb, wx_ref[...]) + bx_ref[...])
    log_a = -RG_C * r * _softplus(-lam_ref[...])
    a = jnp.exp(log_a)
    u = jnp.sqrt(-jnp.tanh(log_a) * (1.0 + a * a)) * (ig * xc)
    row = lax.broadcasted_iota(jnp.int32, a.shape, 0)
    d = 1
    while d < tm:
        a_sh = pltpu.roll(a, d, axis=0)
        u_sh = pltpu.roll(u, d, axis=0)
        keep = row >= d
        u = jnp.where(keep, a * u_sh + u, u)
        a = jnp.where(keep, a * a_sh, a)
        d *= 2
    h = a * h_ref[0:1, :] + u
    o_ref[...] = (h * gg_ref[...]).astype(o_ref.dtype)
    last = h[tm - 1:tm, :]
    h_ref[...] = jnp.broadcast_to(last, h_ref.shape)
    hl_ref[...] = last


def _rglru(x, gg, hist0, h0, w, b, wa, ba, wx, bx, lam):
    bsz, t, c = x.shape
    tm = min(SEQ_TILE, t)
    nt = t // tm
    has_prev = nt > 1
    halo = CONVC_HALO
    r = tm // halo if has_prev else 1
    tile = pl.BlockSpec((None, tm, c), lambda bi, i: (bi, i, 0))
    in_specs = [tile]
    args = [x]
    if has_prev:
        in_specs.append(pl.BlockSpec((None, halo, c), lambda bi, i: (bi, jnp.maximum(i * r - 1, 0), 0)))
        args.append(x)
    in_specs += [pl.BlockSpec((None, halo, c), lambda bi, i: (bi, 0, 0)), tile,
                 pl.BlockSpec((None, 1, c), lambda bi, i: (bi, 0, 0))]
    args += [hist0, gg, h0]
    consts = [w, b, wa, ba, wx, bx, lam]
    in_specs += [_const_spec(a.shape) for a in consts]
    args += consts
    return pl.pallas_call(
        functools.partial(_rglru_body, tm=tm, has_prev=has_prev), grid=(bsz, nt),
        in_specs=in_specs,
        out_specs=(tile, pl.BlockSpec((None, 1, c), lambda bi, i: (bi, 0, 0))),
        out_shape=(jax.ShapeDtypeStruct((bsz, t, c), BF16), jax.ShapeDtypeStruct((bsz, 1, c), F32)),
        scratch_shapes=[pltpu.VMEM((halo + tm, c), F32), pltpu.VMEM((SUBLANES, c), F32)],
        compiler_params=_cparams(("parallel", "arbitrary"), 24), name="rglru")(*args)


def _merge_body(x_ref, oa_ref, ub_ref, hc_ref, gates_ref, wa_ref, wb_ref, bb_ref, wc_ref, wo_ref, g_ref, b_ref,
                wr_ref, wrl_ref, br_ref, x1p_ref, idx_ref, tw_ref, *, alpha):
    d = x_ref.shape[1]
    ya = _dot(oa_ref[...], wa_ref[...])
    yb = _dot(ub_ref[...], wb_ref[...]) + bb_ref[...]
    yc = _dot(hc_ref[...], wc_ref[...])
    merged = gates_ref[:, 0:d] * ya + gates_ref[:, d:2 * d] * yb + gates_ref[:, 2 * d:3 * d] * yc
    x1 = _layer_norm(alpha * x_ref[...] + _dot(merged.astype(BF16), wo_ref[...]), g_ref[...], b_ref[...])
    for p in range(x1p_ref.shape[0]):
        x1p_ref[p] = x1[:, p * PLANE_WIDTH:(p + 1) * PLANE_WIDTH]
    x_hi = x1.astype(BF16)
    x_lo = (x1 - x_hi.astype(F32)).astype(BF16)
    logits = (_dot(x_hi, wr_ref[...]) + (_dot(x_lo, wr_ref[...]) + _dot(x_hi, wrl_ref[...]))) + br_ref[...]
    lane = lax.broadcasted_iota(jnp.int32, logits.shape, 1)
    vals = logits
    idx_out = jnp.zeros(logits.shape, jnp.int32)
    val_out = jnp.full(logits.shape, NEG_BIG, F32)
    for k in range(TOP_K):
        m = jnp.max(vals, axis=-1, keepdims=True)
        idx = jnp.min(jnp.where(vals == m, lane, LANES), axis=-1, keepdims=True)
        idx_out = jnp.where(lane == k, idx, idx_out)
        val_out = jnp.where(lane == k, m, val_out)
        vals = jnp.where(lane == idx, -jnp.inf, vals)
    e = jnp.exp(val_out - jnp.max(val_out, axis=-1, keepdims=True))
    e = jnp.where(lane < TOP_K, e, 0.0)
    idx_ref[...] = idx_out[:, :TOP_K]
    tw_ref[...] = (e / jnp.sum(e, axis=-1, keepdims=True))[:, :TOP_K]


def _merge(x, oa, ub, hc, gates, wts, alpha):
    n, d = x.shape
    tm = min(MERGE_TILE, n)

    def row(c):
        return pl.BlockSpec((tm, c), lambda i: (i, 0))

    return pl.pallas_call(
        functools.partial(_merge_body, alpha=alpha), grid=(n // tm,),
        in_specs=[row(d), row(oa.shape[1]), row(ub.shape[1]), row(hc.shape[1]), row(gates.shape[1])]
        + [_const_spec(a.shape) for a in wts],
        out_specs=(pl.BlockSpec((d // PLANE_WIDTH, tm, PLANE_WIDTH), lambda i: (0, i, 0)), row(TOP_K), row(TOP_K)),
        out_shape=(jax.ShapeDtypeStruct((d // PLANE_WIDTH, n, PLANE_WIDTH), F32),
                   jax.ShapeDtypeStruct((n, TOP_K), jnp.int32), jax.ShapeDtypeStruct((n, TOP_K), F32)),
        compiler_params=_cparams(("parallel",), 40), name="merge_router")(x, oa, ub, hc, gates, *wts)


def _moe_body(te_ref, tf_ref, tr_ref, xs_ref, wgu_ref, bgu_ref, wd_ref, bd_ref, ys_ref, wgub_ref, wdb_ref):
    del te_ref
    i = pl.program_id(0)
    n_planes, tm, pw = xs_ref.shape

    @pl.when(tf_ref[i] == 1)
    def _():
        wgub_ref[...] = wgu_ref[...].astype(BF16)
        wdb_ref[...] = wd_ref[...].astype(BF16)

    @pl.when(tr_ref[i] > 0)
    def _():
        dff = wd_ref.shape[0]
        live = lax.broadcasted_iota(jnp.int32, (tm, pw), 0) < tr_ref[i]
        gu = bgu_ref[...]
        for p in range(n_planes):
            xp = jnp.where(live, xs_ref[p], 0.0).astype(BF16)
            gu = gu + _dot(xp, wgub_ref[p * pw:(p + 1) * pw, :])
        gate = jnp.minimum(gu[:, :dff], SWIGLU_LIMIT)
        up = jnp.clip(gu[:, dff:], -SWIGLU_LIMIT, SWIGLU_LIMIT)
        hid = (up + 1.0) * gate * jax.nn.sigmoid(SWIGLU_ALPHA * gate)
        out = _dot(hid.astype(BF16), wdb_ref[...]) + bd_ref[...]
        for p in range(n_planes):
            ys_ref[p] = out[:, p * pw:(p + 1) * pw]

    @pl.when(tr_ref[i] == 0)
    def _():
        ys_ref[...] = jnp.zeros_like(ys_ref)


def _moe_experts(layer, tile_expert, tile_first, tile_rows, xs, w_gate_up, b_gate_up, w_down, b_down):
    n_planes, r, pw = xs.shape
    d = n_planes * pw
    tm = MOE_TILE
    n_exp, _, dgu = w_gate_up.shape[1:]
    dff = w_down.shape[2]
    bgu = b_gate_up.reshape(-1, 1, dgu)
    bd = b_down.reshape(-1, 1, d)
    grid_spec = pltpu.PrefetchScalarGridSpec(
        num_scalar_prefetch=3, grid=(r // tm,),
        in_specs=[pl.BlockSpec((n_planes, tm, pw), lambda i, te, tf, tr: (0, i, 0)),
                  pl.BlockSpec((None, None, d, dgu), lambda i, te, tf, tr: (layer, te[i], 0, 0)),
                  pl.BlockSpec((None, 1, dgu), lambda i, te, tf, tr: (layer * n_exp + te[i], 0, 0)),
                  pl.BlockSpec((None, None, dff, d), lambda i, te, tf, tr: (layer, te[i], 0, 0)),
                  pl.BlockSpec((None, 1, d), lambda i, te, tf, tr: (layer * n_exp + te[i], 0, 0))],
        out_specs=pl.BlockSpec((n_planes, tm, pw), lambda i, te, tf, tr: (0, i, 0)),
        scratch_shapes=[pltpu.VMEM((d, dgu), BF16), pltpu.VMEM((dff, d), BF16)])
    return pl.pallas_call(
        _moe_body, grid_spec=grid_spec, out_shape=jax.ShapeDtypeStruct((n_planes, r, pw), F32),
        compiler_params=_cparams(("arbitrary",), 56), name="moe_experts")(
            tile_expert, tile_first, tile_rows, xs, w_gate_up, bgu, w_down, bd)


def _combine_body(x1p_ref, yg_ref, tw_ref, g_ref, b_ref, o_ref, *, alpha):
    tw = tw_ref[...]
    cols = []
    for p in range(x1p_ref.shape[0]):
        acc = alpha * x1p_ref[p]
        for k in range(TOP_K):
            acc = acc + tw[:, k:k + 1] * yg_ref[k, p]
        cols.append(acc)
    o_ref[...] = _layer_norm(jnp.concatenate(cols, axis=1), g_ref[...], b_ref[...])


def _combine(x1p, yg, tw, g, b, alpha):
    n_planes, n, pw = x1p.shape
    d = n_planes * pw
    tm = TOKEN_TILE
    return pl.pallas_call(
        functools.partial(_combine_body, alpha=alpha), grid=(n // tm,),
        in_specs=[pl.BlockSpec((n_planes, tm, pw), lambda i: (0, i, 0)),
                  pl.BlockSpec((TOP_K, n_planes, tm, pw), lambda i: (0, 0, i, 0)),
                  pl.BlockSpec((tm, TOP_K), lambda i: (i, 0)), _const_spec(g.shape), _const_spec(b.shape)],
        out_specs=pl.BlockSpec((tm, d), lambda i: (i, 0)), out_shape=jax.ShapeDtypeStruct((n, d), F32),
        compiler_params=_cparams(("parallel",), 32), name="moe_combine_norm")(x1p, yg, tw, g, b)


def _dispatch_body(idx_ref, pos_ref, meta_ref, cnt_ref, base_ref):
    ph = pl.program_id(0)
    i = pl.program_id(1)
    tb = idx_ref.shape[0]
    lane = lax.broadcasted_iota(jnp.int32, (tb, LANES), 1)
    idx = idx_ref[...]
    onehot = [(idx[:, k:k + 1] == lane).astype(F32) for k in range(TOP_K)]
    colsum = [jnp.sum(o, axis=0, keepdims=True) for o in onehot]

    @pl.when(jnp.logical_and(ph == 0, i == 0))
    def _():
        cnt_ref[...] = jnp.zeros_like(cnt_ref)

    @pl.when(ph == 0)
    def _():
        cnt_ref[...] = cnt_ref[...] + (colsum[0] + colsum[1] + colsum[2] + colsum[3])

    @pl.when(jnp.logical_and(ph == 1, i == 0))
    def _():
        cnt = cnt_ref[...]
        ptiles = jnp.floor((cnt + (MOE_TILE - 1)) * (1.0 / MOE_TILE))
        tile_end = _lane_cumsum(ptiles)
        base_ref[...] = (tile_end - ptiles) * MOE_TILE
        row = lax.broadcasted_iota(jnp.int32, meta_ref.shape, 0)
        meta_ref[...] = jnp.where(row == 0, tile_end[0:1, :], jnp.where(row == 1, cnt[0:1, :], 0.0)).astype(jnp.int32)

    @pl.when(ph == 1)
    def _():
        earlier = (lax.broadcasted_iota(jnp.int32, (tb, tb), 0) > lax.broadcasted_iota(jnp.int32, (tb, tb), 1))
        earlier = earlier.astype(BF16)
        base = base_ref[0:1, :]
        slot = lax.broadcasted_iota(jnp.int32, (tb, TOP_K), 1)
        pos = jnp.zeros((tb, TOP_K), F32)
        for k in range(TOP_K):
            seen = _dot(earlier, onehot[k].astype(BF16)) + base
            pos = jnp.where(slot == k, jnp.sum(onehot[k] * seen, axis=1, keepdims=True), pos)
            base = base + colsum[k]
        base_ref[...] = jnp.broadcast_to(base, base_ref.shape)
        pos_ref[...] = pos.astype(jnp.int32)


def _dispatch(top_idx):
    n = top_idx.shape[0]
    tm = MOE_TILE
    n_tiles = -(-(n * TOP_K) // tm) + N_EXPERTS
    tb = next(t for t in DISPATCH_TILES if n % t == 0)
    pos, meta = pl.pallas_call(
        _dispatch_body, grid=(2, n // tb),
        in_specs=[pl.BlockSpec((tb, TOP_K), lambda ph, i: (i, 0))],
        out_specs=(pl.BlockSpec((tb, TOP_K), lambda ph, i: (i * ph, 0)), _const_spec((SUBLANES, LANES))),
        out_shape=(jax.ShapeDtypeStruct((n, TOP_K), jnp.int32), jax.ShapeDtypeStruct((SUBLANES, LANES), jnp.int32)),
        scratch_shapes=[pltpu.VMEM((SUBLANES, LANES), F32), pltpu.VMEM((SUBLANES, LANES), F32)],
        compiler_params=_cparams(("arbitrary", "arbitrary"), 32), name="moe_dispatch")(top_idx)
    tile_end, cnt = meta[0, :N_EXPERTS], meta[1, :N_EXPERTS]
    tiles = jnp.arange(n_tiles, dtype=jnp.int32)
    te = jnp.minimum(jnp.sum((tiles[:, None] >= tile_end[None, :]).astype(jnp.int32), axis=1), N_EXPERTS - 1)
    tile_start = tile_end - (cnt + tm - 1) // tm
    tr = jnp.clip(cnt[te] - (tiles - tile_start[te]) * tm, 0, tm) * (tiles < tile_end[-1]).astype(jnp.int32)
    tf = jnp.concatenate([jnp.ones((1,), jnp.int32), (te[1:] != te[:-1]).astype(jnp.int32)])
    return pos, te, tf, tr, n_tiles * tm


def _sc_mesh():
    return plsc.VectorSubcoreMesh(core_axis_name="c", subcore_axis_name="s")


def _sc_gather_rows(table, idx):
    m = idx.shape[0]
    w = table.shape[1]

    @functools.partial(pl.kernel, out_type=jax.ShapeDtypeStruct((m, w), table.dtype), mesh=_sc_mesh())
    def gather(t_hbm, i_hbm, o_hbm):
        def body(i_vmem, o_vmem):
            pltpu.sync_copy(t_hbm.at[i_vmem.at[0]], o_vmem)

        pltpu.emit_pipeline(
            body, grid=(m // SC_WINDOW,),
            in_specs=[pl.BlockSpec((1, SC_WINDOW), index_map=lambda i: (0, i))],
            out_specs=[pl.BlockSpec((SC_WINDOW, w), index_map=lambda i: (i, 0))],
            core_axis_name=("c", "s"), dimension_semantics=(pltpu.PARALLEL,))(i_hbm, o_hbm)

    return gather(table, idx.reshape(1, m))


def _sc_scatter_rows(rows, idx, n_out):
    n_src, w = rows.shape
    m = idx.shape[0]
    src_blocks = n_src // SC_WINDOW

    @functools.partial(pl.kernel, out_type=jax.ShapeDtypeStruct((n_out, w), rows.dtype), mesh=_sc_mesh(),
                       scratch_types=[])
    def scatter(x_hbm, i_hbm, o_hbm):
        def body(x_vmem, i_vmem):
            pltpu.sync_copy(x_vmem, o_hbm.at[i_vmem.at[0]])

        pltpu.emit_pipeline(
            body, grid=(m // SC_WINDOW,),
            in_specs=[pl.BlockSpec((SC_WINDOW, w), index_map=lambda i: (lax.rem(i, src_blocks), 0)),
                      pl.BlockSpec((1, SC_WINDOW), index_map=lambda i: (0, i))],
            out_specs=[], core_axis_name=("c", "s"), dimension_semantics=(pltpu.PARALLEL,))(x_hbm, i_hbm)

    return scatter(rows, idx.reshape(1, m))


def _plane_rows(pos, n_planes, rows_per_plane):
    off = jnp.arange(n_planes, dtype=jnp.int32) * rows_per_plane
    return (pos.T[:, None, :] + off[None, :, None]).reshape(-1)


def _pad_rows(a, rows):
    return jnp.pad(a, ((0, 0), (rows - a.shape[1], 0), (0, 0)))


def _block_diag(w):
    h, bd, _ = w.shape
    eye = jnp.eye(h, dtype=w.dtype)
    return jnp.einsum("hij,hg->higj", w, eye).reshape(h * bd, h * bd)


def kernel(x_prompt, x_sample, cache_k, cache_v, cache_logf, state_conv_b, state_conv_c, state_h, page_table,
           w_in, b_f, w_att_o, conv_b_w, conv_b_b, lnb_g, lnb_b, w_conv_o, b_conv_o, conv_c_w, conv_c_b,
           w_rg_a, b_rg_a, w_rg_x, b_rg_x, rg_lambda, w_rg_o, w_out, ln1_g, ln1_b, w_router, b_router,
           w_gate_up, b_gate_up, w_down, b_down, ln2_g, ln2_b):
    bp, seq, d = x_prompt.shape
    bs, ns, _ = x_sample.shape
    depth, n_pool, page = cache_k.shape[:3]
    w_conv = state_conv_b.shape[-1]
    w_rg = state_conv_c.shape[-1]
    alpha = (2 * depth) ** 0.25
    n_p, n_s = bp * seq, bs * ns
    off_f = 3 * W_ATT
    off_glu = off_f + H_ATT
    off_rg = off_glu + 2 * w_conv
    off_gate = off_rg + 2 * w_rg

    ckt = cache_k.transpose(0, 1, 3, 4, 2).reshape(depth, n_pool, W_ATT, page)
    cvt = cache_v.transpose(0, 1, 3, 4, 2).reshape(depth, n_pool, W_ATT, page)
    cft = cache_logf.transpose(0, 1, 3, 2)
    head_eye = jnp.eye(H_ATT, dtype=BF16)

    xp = x_prompt.reshape(n_p, d)
    xs = x_sample.reshape(n_s, d)
    outs = {k: [] for k in ("kp", "vp", "fp", "ks", "vs", "fs", "cbp", "cbs", "ccp", "ccs", "hp", "hs")}

    for l in range(depth):
        wl = w_in[l]
        wf = wl[:, off_f:off_glu]
        in_w = (wl[:, :2 * W_ATT].astype(BF16), wl[:, :off_f].T.astype(BF16),
                jnp.pad(wf, ((0, 0), (0, LANES - H_ATT))).astype(BF16), wf.T.astype(BF16),
                b_f[l].reshape(1, H_ATT), b_f[l].reshape(H_ATT, 1),
                wl[:, off_glu:off_rg].astype(BF16), wl[:, off_rg:off_gate].astype(BF16),
                wl[:, off_gate:].astype(BF16))
        cbw = jnp.broadcast_to(conv_b_w[l][:, None, :], (CONV_WIDTH, SUBLANES, w_conv))
        ccw = jnp.pad(conv_c_w[l], ((0, CONVC_HALO - RG_CONV_WIDTH), (0, 0)))
        rg_w = (ccw, conv_c_b[l].reshape(1, -1), _block_diag(w_rg_a[l]).astype(BF16), b_rg_a[l].reshape(1, -1),
                _block_diag(w_rg_x[l]).astype(BF16), b_rg_x[l].reshape(1, -1), rg_lambda[l].reshape(1, -1))
        cb_w = (cbw, conv_b_b[l].reshape(1, -1), lnb_g[l].reshape(1, -1), lnb_b[l].reshape(1, -1))
        wr = jnp.pad(w_router[l], ((0, 0), (0, LANES - N_EXPERTS)))
        br = jnp.pad(b_router[l], (0, LANES - N_EXPERTS), constant_values=NEG_BIG).reshape(1, LANES)
        mg_w = (w_att_o[l].astype(BF16), w_conv_o[l].astype(BF16), b_conv_o[l].reshape(1, -1),
                w_rg_o[l].astype(BF16), w_out[l].astype(BF16), ln1_g[l].reshape(1, -1), ln1_b[l].reshape(1, -1),
                wr.astype(BF16), (wr - wr.astype(BF16).astype(F32)).astype(BF16), br)

        (q, kb, qtm, kt, vt, vtb, lf, lft, u, rgx, gg, gates) = _inproj(xp, in_w, seq)
        oa = _fox_prompt(qtm, kb, vtb, _cumsum(lf, seq), bp, seq)
        u3 = u.reshape(bp, seq, w_conv)
        ub = _convb(u3, jnp.zeros((bp, CONVB_HALO, w_conv), F32), *cb_w)
        rgx3 = rgx.reshape(bp, seq, w_rg)
        hc, hl = _rglru(rgx3, gg.reshape(bp, seq, w_rg), jnp.zeros((bp, CONVC_HALO, w_rg), F32),
                        jnp.zeros((bp, 1, w_rg), F32), *rg_w)
        x1p, tip, twp = _merge(xp, oa, ub.reshape(n_p, w_conv), hc.reshape(n_p, w_rg), gates, mg_w, alpha)
        outs["kp"].append(kt.reshape(bp, H_ATT, HD_ATT, seq).transpose(0, 3, 1, 2))
        outs["vp"].append(vt.reshape(bp, H_ATT, HD_ATT, seq).transpose(0, 3, 1, 2))
        outs["fp"].append(lft.transpose(0, 2, 1))
        outs["cbp"].append(u3[:, seq - (CONV_WIDTH - 1):])
        outs["ccp"].append(rgx3[:, seq - (RG_CONV_WIDTH - 1):])
        outs["hp"].append(hl.reshape(bp, w_rg))

        (q, kb, qtm, kt, vt, vtb, lf, lft, u, rgx, gg, gates) = _inproj(xs, in_w, n_s)
        qbd = jnp.einsum("bthd,hg->bhtgd", q.reshape(bs, ns, H_ATT, HD_ATT), head_eye)
        qbd = qbd.reshape(bs, H_ATT * ns, W_ATT)

        def new_page(a):
            a = a.reshape(a.shape[0], bs, ns).transpose(1, 0, 2)
            return jnp.pad(a, ((0, 0), (0, 0), (0, page - ns)))

        oa = _fox_sample(l, page_table, qbd, ckt, cvt, cft, new_page(kt[0]), new_page(vt[0]), new_page(lft[0]), ns)
        u3 = u.reshape(bs, ns, w_conv)
        ub = _convb(u3, _pad_rows(state_conv_b[l], CONVB_HALO), *cb_w)
        rgx3 = rgx.reshape(bs, ns, w_rg)
        hc, hl = _rglru(rgx3, gg.reshape(bs, ns, w_rg), _pad_rows(state_conv_c[l], CONVC_HALO),
                        state_h[l].reshape(bs, 1, w_rg), *rg_w)
        x1s, tis, tws = _merge(xs, oa.reshape(n_s, W_ATT), ub.reshape(n_s, w_conv), hc.reshape(n_s, w_rg),
                                     gates, mg_w, alpha)
        outs["ks"].append(kt[0].T.reshape(bs, ns, H_ATT, HD_ATT))
        outs["vs"].append(vt[0].T.reshape(bs, ns, H_ATT, HD_ATT))
        outs["fs"].append(lf.reshape(bs, ns, H_ATT))
        outs["cbs"].append(jnp.concatenate([state_conv_b[l], u3], axis=1)[:, ns:])
        outs["ccs"].append(jnp.concatenate([state_conv_c[l], rgx3], axis=1)[:, ns:])
        outs["hs"].append(hl.reshape(bs, w_rg))

        pos, te, tf, tr, n_rows = _dispatch(jnp.concatenate([tip, tis], axis=0))
        x1_all = jnp.concatenate([x1p, x1s], axis=1)
        n_planes, n_all, pw = x1_all.shape
        xs_rows = _sc_scatter_rows(x1_all.reshape(n_planes * n_all, pw), _plane_rows(pos, n_planes, n_rows),
                                   n_planes * n_rows)
        ys = _moe_experts(l, te, tf, tr, xs_rows.reshape(n_planes, n_rows, pw), w_gate_up, b_gate_up, w_down, b_down)
        ys = ys.reshape(n_planes * n_rows, pw)
        g2, b2 = ln2_g[l].reshape(1, -1), ln2_b[l].reshape(1, -1)
        yg = _sc_gather_rows(ys, _plane_rows(pos[:n_p], n_planes, n_rows)).reshape(TOP_K, n_planes, n_p, pw)
        xp = _combine(x1p, yg, twp, g2, b2, alpha)
        yg = _sc_gather_rows(ys, _plane_rows(pos[n_p:], n_planes, n_rows)).reshape(TOP_K, n_planes, n_s, pw)
        xs = _combine(x1s, yg, tws, g2, b2, alpha)

    st = {k: jnp.stack(v, 0) for k, v in outs.items()}
    return (xp.reshape(bp, seq, d), xs.reshape(bs, ns, d), st["kp"], st["vp"], st["fp"], st["ks"], st["vs"],
            st["fs"], st["cbp"], st["cbs"], st["ccp"], st["ccs"], st["hp"], st["hs"])
```

```python
import functools
import math

import jax
import jax.numpy as jnp
from jax import lax
from jax.experimental import pallas as pl
from jax.experimental.pallas import tpu as pltpu
from jax.experimental.pallas import tpu_sc as plsc

F32 = jnp.float32
BF16 = jnp.bfloat16

H_ATT = 8
HD_ATT = 64
W_ATT = H_ATT * HD_ATT
CONV_WIDTH = 31
RG_CONV_WIDTH = 4
H_RG = 8
RG_C = 8.0
N_EXPERTS = 32
TOP_K = 4
SWIGLU_LIMIT = 7.0
SWIGLU_ALPHA = 1.702
LN_EPS = 1e-5
N_BRANCH = 3

LANES = 128
SUBLANES = 8
NEG_BIG = -1e30

TOKEN_TILE = 256
MERGE_TILE = 512
ATT_TILE = 512
CUMSUM_TILE = 512
SEQ_TILE = 256
CONVB_HALO = 32
CONVC_HALO = 8
CONV_CHUNK = 32
PAGES_PER_STEP = 16
MOE_TILE = 512
PLANE_WIDTH = 256
SC_WINDOW = 128
DISPATCH_TILES = (1280, 1024, 768, 512, 256)


def _cparams(sem, vmem_mb):
    return pltpu.CompilerParams(dimension_semantics=sem, vmem_limit_bytes=vmem_mb << 20)


def _const_spec(shape):
    nd = len(shape)
    return pl.BlockSpec(shape, lambda *_: (0,) * nd)


def _log_sigmoid(z):
    return jnp.minimum(z, 0.0) - jnp.log1p(jnp.exp(-jnp.abs(z)))


def _softplus(z):
    return jnp.maximum(z, 0.0) + jnp.log1p(jnp.exp(-jnp.abs(z)))


def _layer_norm(x, g, b):
    mu = jnp.mean(x, axis=-1, keepdims=True)
    xc = x - mu
    var = jnp.mean(xc * xc, axis=-1, keepdims=True)
    return xc * lax.rsqrt(var + LN_EPS) * g + b


def _dot(a, b):
    return jnp.dot(a, b, preferred_element_type=F32)


def _dot_nt(a, b, precision=None):
    return lax.dot_general(a, b, (((1,), (1,)), ((), ())), preferred_element_type=F32, precision=precision)


def _inproj_body(x_ref, wqk_ref, wt_ref, wf_ref, wft_ref, bf_ref, bft_ref, wglu_ref, wrg_ref, wgate_ref,
                 q_ref, kb_ref, qtm_ref, kt_ref, vt_ref, vtb_ref, lf_ref, lft_ref, u_ref, rgx_ref, gg_ref, gates_ref):
    xb = x_ref[...].astype(BF16)
    w = W_ATT
    scale = HD_ATT ** -0.5
    qk = _dot(xb, wqk_ref[...])
    q_ref[...] = (qk[:, :w] * scale).astype(BF16)
    kb_ref[...] = qk[:, w:].astype(BF16)
    t = _dot_nt(wt_ref[...], xb)
    kt_ref[...] = t[w:2 * w, :]
    vt = t[2 * w:, :]
    vt_ref[...] = vt
    vtb_ref[...] = vt.astype(BF16)
    qt = (t[:w, :] * scale).astype(BF16)
    pair = 2 * HD_ATT
    lower = lax.broadcasted_iota(jnp.int32, (pair, qt.shape[1]), 0) < HD_ATT
    for h in range(H_ATT):
        slab = qt[(h // 2) * pair:(h // 2 + 1) * pair, :]
        qtm_ref[h] = jnp.where(lower if h % 2 == 0 else jnp.logical_not(lower), slab, jnp.zeros_like(slab))
    zf = _dot(xb, wf_ref[...])
    lf_ref[...] = _log_sigmoid(zf[:, :H_ATT] + bf_ref[...])
    lft_ref[...] = _log_sigmoid(_dot_nt(wft_ref[...], xb) + bft_ref[...])
    glu = _dot(xb, wglu_ref[...])
    c = glu.shape[1] // 2
    u_ref[...] = glu[:, :c] * jax.nn.sigmoid(glu[:, c:])
    rg = _dot(xb, wrg_ref[...])
    c = rg.shape[1] // 2
    rgx_ref[...] = rg[:, :c]
    gg_ref[...] = jax.nn.gelu(rg[:, c:]).astype(gg_ref.dtype)
    d = gates_ref.shape[1] // N_BRANCH
    for j in range(N_BRANCH):
        gate = jax.nn.sigmoid(_dot(xb, wgate_ref[:, j * d:(j + 1) * d]))
        gates_ref[:, j * d:(j + 1) * d] = gate.astype(gates_ref.dtype)


def _inproj(x, wts, seq):
    n, d = x.shape
    tm = TOKEN_TILE
    wglu, wrg, wgate = wts[-3:]
    wc, wr = wglu.shape[1] // 2, wrg.shape[1] // 2
    bps = seq // tm

    def row(c):
        return pl.BlockSpec((tm, c), lambda i: (i, 0))

    def col(r):
        return pl.BlockSpec((None, r, tm), lambda i: (i // bps, 0, i % bps))

    def colshape(r, dt):
        return jax.ShapeDtypeStruct((n // seq, r, seq), dt)

    out_shape = (
        jax.ShapeDtypeStruct((n, W_ATT), BF16), jax.ShapeDtypeStruct((n, W_ATT), BF16),
        jax.ShapeDtypeStruct((H_ATT, 2 * HD_ATT, n), BF16),
        colshape(W_ATT, F32), colshape(W_ATT, F32), colshape(W_ATT, BF16),
        jax.ShapeDtypeStruct((n, H_ATT), F32), colshape(H_ATT, F32), jax.ShapeDtypeStruct((n, wc), F32),
        jax.ShapeDtypeStruct((n, wr), F32), jax.ShapeDtypeStruct((n, wr), BF16),
        jax.ShapeDtypeStruct((n, wgate.shape[1]), BF16))
    out_specs = (row(W_ATT), row(W_ATT), pl.BlockSpec((H_ATT, 2 * HD_ATT, tm), lambda i: (0, 0, i)),
                 col(W_ATT), col(W_ATT), col(W_ATT),
                 row(H_ATT), col(H_ATT), row(wc), row(wr), row(wr), row(wgate.shape[1]))
    return pl.pallas_call(
        _inproj_body, grid=(n // tm,),
        in_specs=[row(d)] + [_const_spec(a.shape) for a in wts],
        out_specs=out_specs, out_shape=out_shape,
        compiler_params=_cparams(("parallel",), 56), name="inproj")(x, *wts)


def _lane_cumsum(c):
    n = c.shape[1]
    lane = lax.broadcasted_iota(jnp.int32, c.shape, 1)
    d = 1
    while d < n:
        c = c + jnp.where(lane >= d, pltpu.roll(c, d, axis=1), 0.0)
        d *= 2
    return c


def _cumsum_body(f_ref, c_ref, carry_ref, *, blocks_per_seq):
    @pl.when(pl.program_id(0) % blocks_per_seq == 0)
    def _():
        carry_ref[...] = jnp.zeros_like(carry_ref)

    c = f_ref[...]
    n = c.shape[0]
    row = lax.broadcasted_iota(jnp.int32, c.shape, 0)
    d = 1
    while d < n:
        c = c + jnp.where(row >= d, pltpu.roll(c, d, axis=0), 0.0)
        d *= 2
    c = c + carry_ref[0:1, :]
    carry_ref[...] = jnp.broadcast_to(c[n - 1:n, :], carry_ref.shape)
    for h in range(H_ATT):
        c_ref[h] = jnp.broadcast_to(c[:, h:h + 1], (n, LANES))


def _cumsum(lf, seq):
    n, h = lf.shape
    tc = min(CUMSUM_TILE, seq)
    return pl.pallas_call(
        functools.partial(_cumsum_body, blocks_per_seq=seq // tc), grid=(n // tc,),
        in_specs=[pl.BlockSpec((tc, h), lambda i: (i, 0))],
        out_specs=pl.BlockSpec((h, tc, LANES), lambda i: (0, i, 0)),
        out_shape=jax.ShapeDtypeStruct((h, n, LANES), F32),
        scratch_shapes=[pltpu.VMEM((SUBLANES, h), F32)],
        compiler_params=_cparams(("arbitrary",), 16), name="logf_cumsum")(lf)


def _fox_prompt_body(qi_ref, ki_ref, qtm_ref, k_ref, vt_ref, ck_ref, cq_ref, o_ref, m_ref, l_ref, acc_ref):
    qi = qi_ref[pl.program_id(1)]
    ki = ki_ref[pl.program_id(1)]
    tk, tq = k_ref.shape[0], qtm_ref.shape[2]
    pair = 2 * HD_ATT

    @pl.when(ki == 0)
    def _():
        m_ref[...] = jnp.full_like(m_ref, NEG_BIG)
        l_ref[...] = jnp.zeros_like(l_ref)
        acc_ref[...] = jnp.zeros_like(acc_ref)

    def update(diagonal):
        if diagonal:
            keep = lax.broadcasted_iota(jnp.int32, (tk, tq), 0) <= lax.broadcasted_iota(jnp.int32, (tk, tq), 1)
        for h in range(H_ATT):
            s = _dot(k_ref[:, (h // 2) * pair:(h // 2 + 1) * pair], qtm_ref[h])
            bias = ck_ref[h] - cq_ref[h, 0:1, :]
            s = s - jnp.concatenate([bias] * (tq // LANES), axis=1)
            if diagonal:
                s = jnp.where(keep, s, NEG_BIG)
            m_old = m_ref[h]
            m_new = jnp.maximum(m_old, jnp.max(s, axis=0, keepdims=True))
            p = jnp.exp(s - m_new)
            alpha = jnp.exp(m_old - m_new)
            l_ref[h] = alpha * l_ref[h] + jnp.sum(p, axis=0, keepdims=True)
            acc_ref[h] = alpha * acc_ref[h] + _dot(vt_ref[h * HD_ATT:(h + 1) * HD_ATT, :], p.astype(BF16))
            m_ref[h] = m_new

    @pl.when(ki < qi)
    def _():
        update(False)

    @pl.when(ki == qi)
    def _():
        update(True)
        for j in range(H_ATT // 2):
            o2 = jnp.concatenate([acc_ref[2 * j] / l_ref[2 * j], acc_ref[2 * j + 1] / l_ref[2 * j + 1]], axis=0)
            o_ref[:, j * pair:(j + 1) * pair] = o2.T.astype(o_ref.dtype)


def _fox_prompt(qtm, kb, vt, cb, batch, seq):
    n = kb.shape[0]
    t = min(ATT_TILE, seq)
    nb = seq // t
    pairs = [(qi, ki) for qi in range(nb) for ki in range(qi + 1)]
    qi_of = jnp.array([p[0] for p in pairs], jnp.int32)
    ki_of = jnp.array([p[1] for p in pairs], jnp.int32)
    grid_spec = pltpu.PrefetchScalarGridSpec(
        num_scalar_prefetch=2, grid=(batch, len(pairs)),
        in_specs=[pl.BlockSpec((H_ATT, 2 * HD_ATT, t), lambda b, p, qs, ks: (0, 0, b * nb + qs[p])),
                  pl.BlockSpec((t, W_ATT), lambda b, p, qs, ks: (b * nb + ks[p], 0)),
                  pl.BlockSpec((None, W_ATT, t), lambda b, p, qs, ks: (b, 0, ks[p])),
                  pl.BlockSpec((H_ATT, t, LANES), lambda b, p, qs, ks: (0, b * nb + ks[p], 0)),
                  pl.BlockSpec((H_ATT, SUBLANES, LANES),
                               lambda b, p, qs, ks: (0, (b * nb + qs[p]) * (t // SUBLANES), 0))],
        out_specs=pl.BlockSpec((t, W_ATT), lambda b, p, qs, ks: (b * nb + qs[p], 0)),
        scratch_shapes=[pltpu.VMEM((H_ATT, 1, t), F32), pltpu.VMEM((H_ATT, 1, t), F32),
                        pltpu.VMEM((H_ATT, HD_ATT, t), F32)])
    return pl.pallas_call(
        _fox_prompt_body, grid_spec=grid_spec, out_shape=jax.ShapeDtypeStruct((n, W_ATT), BF16),
        compiler_params=_cparams(("parallel", "arbitrary"), 40), name="fox_prompt")(
            qi_of, ki_of, qtm, kb, vt, cb, cb)


def _fox_sample_body(pt_ref, qbd_ref, *refs, n_pages_step, n_new):
    del pt_ref
    np_ = n_pages_step
    k_refs = refs[:np_]
    v_refs = refs[np_:2 * np_]
    f_refs = refs[2 * np_:3 * np_]
    kn_ref, vn_ref, fn_ref, o_ref, m_ref, l_ref, acc_ref, carry_ref, kb_ref, vb_ref = refs[3 * np_:]
    c = pl.program_id(1)
    rows = qbd_ref.shape[0]
    page = kn_ref.shape[1]

    @pl.when(c == 0)
    def _():
        m_ref[...] = jnp.full_like(m_ref, NEG_BIG)
        l_ref[...] = jnp.zeros_like(l_ref)
        acc_ref[...] = jnp.zeros_like(acc_ref)
        carry_ref[...] = jnp.zeros_like(carry_ref)

    def decay_rows(ft):
        ct = _lane_cumsum(ft) + carry_ref[:, 0:1]
        carry_ref[...] = jnp.broadcast_to(ct[:, ct.shape[1] - 1:], carry_ref.shape)
        return jnp.concatenate([jnp.broadcast_to(ct[h:h + 1, :], (n_new, ct.shape[1])) for h in range(H_ATT)], axis=0)

    def update(kb, vb, cexp, keep):
        s = _dot(qbd_ref[...], kb) - cexp
        if keep is not None:
            s = jnp.where(keep, s, NEG_BIG)
        m_old = m_ref[...]
        m_new = jnp.maximum(m_old, jnp.max(s, axis=-1, keepdims=True))
        p = jnp.exp(s - m_new)
        alpha = jnp.exp(m_old - m_new)
        l_ref[...] = alpha * l_ref[...] + jnp.sum(p, axis=-1, keepdims=True)
        acc_ref[...] = alpha * acc_ref[...] + _dot_nt(p.astype(BF16), vb)
        m_ref[...] = m_new

    for j in range(np_):
        kb_ref[:, j * page:(j + 1) * page] = k_refs[j][...].astype(BF16)
        vb_ref[:, j * page:(j + 1) * page] = v_refs[j][...].astype(BF16)
    ft = jnp.concatenate([r[...] for r in f_refs], axis=1)
    update(kb_ref[...], vb_ref[...], decay_rows(ft), None)

    @pl.when(c == pl.num_programs(1) - 1)
    def _():
        tok = lax.broadcasted_iota(jnp.int32, (rows, page), 0) % n_new
        key = lax.broadcasted_iota(jnp.int32, (rows, page), 1)
        update(kn_ref[...].astype(BF16), vn_ref[...].astype(BF16), decay_rows(fn_ref[...]), key <= tok)
        acc = acc_ref[...] / l_ref[...]
        for h in range(H_ATT):
            o_ref[:, h * HD_ATT:(h + 1) * HD_ATT] = acc[h * n_new:(h + 1) * n_new,
                                                        h * HD_ATT:(h + 1) * HD_ATT].astype(o_ref.dtype)


def _fox_sample(layer, page_table, qbd, cache_kt, cache_vt, cache_ft, kt_new, vt_new, ft_new, n_new):
    bsz, n_pages = page_table.shape
    page = cache_kt.shape[3]
    np_ = PAGES_PER_STEP
    while n_pages % np_:
        np_ //= 2
    rows = qbd.shape[1]

    def page_spec(r, j):
        return pl.BlockSpec((None, None, r, page), lambda b, c, pt: (layer, pt[b, c * np_ + j], 0, 0))

    def seq_spec(r, w):
        return pl.BlockSpec((None, r, w), lambda b, c, pt: (b, 0, 0))

    in_specs = ([seq_spec(rows, W_ATT)]
                + [page_spec(W_ATT, j) for j in range(np_)] + [page_spec(W_ATT, j) for j in range(np_)]
                + [page_spec(H_ATT, j) for j in range(np_)]
                + [seq_spec(W_ATT, page), seq_spec(W_ATT, page), seq_spec(H_ATT, page)])
    grid_spec = pltpu.PrefetchScalarGridSpec(
        num_scalar_prefetch=1, grid=(bsz, n_pages // np_), in_specs=in_specs,
        out_specs=seq_spec(n_new, W_ATT),
        scratch_shapes=[pltpu.VMEM((rows, 1), F32), pltpu.VMEM((rows, 1), F32), pltpu.VMEM((rows, W_ATT), F32),
                        pltpu.VMEM((H_ATT, LANES), F32),
                        pltpu.VMEM((W_ATT, np_ * page), BF16), pltpu.VMEM((W_ATT, np_ * page), BF16)])
    return pl.pallas_call(
        functools.partial(_fox_sample_body, n_pages_step=np_, n_new=n_new),
        grid_spec=grid_spec, out_shape=jax.ShapeDtypeStruct((bsz, n_new, W_ATT), BF16),
        compiler_params=_cparams(("parallel", "arbitrary"), 40), name="fox_sample")(
            page_table, qbd, *([cache_kt] * np_), *([cache_vt] * np_), *([cache_ft] * np_), kt_new, vt_new, ft_new)


def _convb_body(*refs, tm, has_prev):
    if has_prev:
        u_ref, prev_ref, hist0_ref, w_ref, b_ref, g_ref, beta_ref, o_ref, ext_ref = refs
        hist = jnp.where(pl.program_id(1) == 0, hist0_ref[...], prev_ref[...])
    else:
        u_ref, hist0_ref, w_ref, b_ref, g_ref, beta_ref, o_ref, ext_ref = refs
        hist = hist0_ref[...]
    halo = CONVB_HALO
    ext_ref[0:halo, :] = hist
    ext_ref[halo:halo + tm, :] = u_ref[...]
    lead = halo - (CONV_WIDTH - 1)
    ch = min(CONV_CHUNK, tm)
    for r0 in range(0, tm, ch):
        acc = jnp.broadcast_to(b_ref[...], (ch, b_ref.shape[1]))
        for r in range(SUBLANES):
            taps = [k for k in range(CONV_WIDTH) if (k + lead) % SUBLANES == r]
            if not taps:
                continue
            win = ext_ref[r0 + r:r0 + taps[-1] + lead + ch, :]
            for k in taps:
                off = k + lead - r
                wk = jnp.concatenate([w_ref[k]] * (ch // SUBLANES), axis=0)
                acc = acc + wk * win[off:off + ch, :]
        y = _layer_norm(acc, g_ref[...], beta_ref[...])
        o_ref[r0:r0 + ch, :] = (y * jax.nn.sigmoid(y)).astype(o_ref.dtype)


def _convb(u, hist0, w, b, g, beta):
    bsz, t, c = u.shape
    tm = min(SEQ_TILE, t)
    nt = t // tm
    has_prev = nt > 1
    halo = CONVB_HALO
    r = tm // halo if has_prev else 1
    in_specs = [pl.BlockSpec((None, tm, c), lambda bi, i: (bi, i, 0))]
    args = [u]
    if has_prev:
        in_specs.append(pl.BlockSpec((None, halo, c), lambda bi, i: (bi, jnp.maximum(i * r - 1, 0), 0)))
        args.append(u)
    in_specs += [pl.BlockSpec((None, halo, c), lambda bi, i: (bi, 0, 0)),
                 _const_spec(w.shape), _const_spec(b.shape), _const_spec(g.shape), _const_spec(beta.shape)]
    args += [hist0, w, b, g, beta]
    return pl.pallas_call(
        functools.partial(_convb_body, tm=tm, has_prev=has_prev), grid=(bsz, nt),
        in_specs=in_specs, out_specs=pl.BlockSpec((None, tm, c), lambda bi, i: (bi, i, 0)),
        out_shape=jax.ShapeDtypeStruct((bsz, t, c), BF16),
        scratch_shapes=[pltpu.VMEM((halo + tm, c), F32)],
        compiler_params=_cparams(("parallel", "arbitrary"), 24), name="conformer_conv")(*args)


def _rglru_body(*refs, tm, has_prev):
    if has_prev:
        (x_ref, prev_ref, hist0_ref, gg_ref, h0_ref, w_ref, b_ref, wa_ref, ba_ref, wx_ref, bx_ref, lam_ref,
         o_ref, hl_ref, ext_ref, h_ref) = refs
        hist = jnp.where(pl.program_id(1) == 0, hist0_ref[...], prev_ref[...])
    else:
        (x_ref, hist0_ref, gg_ref, h0_ref, w_ref, b_ref, wa_ref, ba_ref, wx_ref, bx_ref, lam_ref,
         o_ref, hl_ref, ext_ref, h_ref) = refs
        hist = hist0_ref[...]

    @pl.when(pl.program_id(1) == 0)
    def _():
        h_ref[...] = jnp.broadcast_to(h0_ref[...], h_ref.shape)

    halo = CONVC_HALO
    ext_ref[0:halo, :] = hist
    ext_ref[halo:halo + tm, :] = x_ref[...]
    lead = halo - (RG_CONV_WIDTH - 1)
    xc = jnp.broadcast_to(b_ref[...], x_ref.shape)
    for k in range(RG_CONV_WIDTH):
        xc = xc + w_ref[k:k + 1, :] * ext_ref[k + lead:k + lead + tm, :]
    xcb = xc.astype(BF16)
    r = jax.nn.sigmoid(_dot(xcb, wa_ref[...]) + ba_ref[...])
    ig = jax.nn.sigmoid(_dot(xcb, wx_ref[...]) + bx_ref[...])
    log_a = -RG_C * r * _softplus(-lam_ref[...])
    a = jnp.exp(log_a)
    u = jnp.sqrt(-jnp.tanh(log_a) * (1.0 + a * a)) * (ig * xc)
    row = lax.broadcasted_iota(jnp.int32, a.shape, 0)
    d = 1
    while d < tm:
        a_sh = pltpu.roll(a, d, axis=0)
        u_sh = pltpu.roll(u, d, axis=0)
        keep = row >= d
        u = jnp.where(keep, a * u_sh + u, u)
        a = jnp.where(keep, a * a_sh, a)
        d *= 2
    h = a * h_ref[0:1, :] + u
    o_ref[...] = (h * gg_ref[...]).astype(o_ref.dtype)
    last = h[tm - 1:tm, :]
    h_ref[...] = jnp.broadcast_to(last, h_ref.shape)
    hl_ref[...] = last


def _rglru(x, gg, hist0, h0, w, b, wa, ba, wx, bx, lam):
    bsz, t, c = x.shape
    tm = min(SEQ_TILE, t)
    nt = t // tm
    has_prev = nt > 1
    halo = CONVC_HALO
    r = tm // halo if has_prev else 1
    tile = pl.BlockSpec((None, tm, c), lambda bi, i: (bi, i, 0))
    in_specs = [tile]
    args = [x]
    if has_prev:
        in_specs.append(pl.BlockSpec((None, halo, c), lambda bi, i: (bi, jnp.maximum(i * r - 1, 0), 0)))
        args.append(x)
    in_specs += [pl.BlockSpec((None, halo, c), lambda bi, i: (bi, 0, 0)), tile,
                 pl.BlockSpec((None, 1, c), lambda bi, i: (bi, 0, 0))]
    args += [hist0, gg, h0]
    consts = [w, b, wa, ba, wx, bx, lam]
    in_specs += [_const_spec(a.shape) for a in consts]
    args += consts
    return pl.pallas_call(
        functools.partial(_rglru_body, tm=tm, has_prev=has_prev), grid=(bsz, nt),
        in_specs=in_specs,
        out_specs=(tile, pl.BlockSpec((None, 1, c), lambda bi, i: (bi, 0, 0))),
        out_shape=(jax.ShapeDtypeStruct((bsz, t, c), BF16), jax.ShapeDtypeStruct((bsz, 1, c), F32)),
        scratch_shapes=[pltpu.VMEM((halo + tm, c), F32), pltpu.VMEM((SUBLANES, c), F32)],
        compiler_params=_cparams(("parallel", "arbitrary"), 24), name="rglru")(*args)


def _merge_body(x_ref, oa_ref, ub_ref, hc_ref, gates_ref, wa_ref, wb_ref, bb_ref, wc_ref, wo_ref, g_ref, b_ref,
                wr_ref, wrl_ref, br_ref, x1p_ref, idx_ref, tw_ref, *, alpha):
    d = x_ref.shape[1]
    ya = _dot(oa_ref[...], wa_ref[...])
    yb = _dot(ub_ref[...], wb_ref[...]) + bb_ref[...]
    yc = _dot(hc_ref[...], wc_ref[...])
    merged = gates_ref[:, 0:d] * ya + gates_ref[:, d:2 * d] * yb + gates_ref[:, 2 * d:3 * d] * yc
    x1 = _layer_norm(alpha * x_ref[...] + _dot(merged.astype(BF16), wo_ref[...]), g_ref[...], b_ref[...])
    for p in range(x1p_ref.shape[0]):
        x1p_ref[p] = x1[:, p * PLANE_WIDTH:(p + 1) * PLANE_WIDTH]
    x_hi = x1.astype(BF16)
    x_lo = (x1 - x_hi.astype(F32)).astype(BF16)
    logits = (_dot(x_hi, wr_ref[...]) + (_dot(x_lo, wr_ref[...]) + _dot(x_hi, wrl_ref[...]))) + br_ref[...]
    lane = lax.broadcasted_iota(jnp.int32, logits.shape, 1)
    vals = logits
    idx_out = jnp.zeros(logits.shape, jnp.int32)
    val_out = jnp.full(logits.shape, NEG_BIG, F32)
    for k in range(TOP_K):
        m = jnp.max(vals, axis=-1, keepdims=True)
        idx = jnp.min(jnp.where(vals == m, lane, LANES), axis=-1, keepdims=True)
        idx_out = jnp.where(lane == k, idx, idx_out)
        val_out = jnp.where(lane == k, m, val_out)
        vals = jnp.where(lane == idx, -jnp.inf, vals)
    e = jnp.exp(val_out - jnp.max(val_out, axis=-1, keepdims=True))
    e = jnp.where(lane < TOP_K, e, 0.0)
    idx_ref[...] = idx_out[:, :TOP_K]
    tw_ref[...] = (e / jnp.sum(e, axis=-1, keepdims=True))[:, :TOP_K]


def _merge(x, oa, ub, hc, gates, wts, alpha):
    n, d = x.shape
    tm = min(MERGE_TILE, n)

    def row(c):
        return pl.BlockSpec((tm, c), lambda i: (i, 0))

    return pl.pallas_call(
        functools.partial(_merge_body, alpha=alpha), grid=(n // tm,),
        in_specs=[row(d), row(oa.shape[1]), row(ub.shape[1]), row(hc.shape[1]), row(gates.shape[1])]
        + [_const_spec(a.shape) for a in wts],
        out_specs=(pl.BlockSpec((d // PLANE_WIDTH, tm, PLANE_WIDTH), lambda i: (0, i, 0)), row(TOP_K), row(TOP_K)),
        out_shape=(jax.ShapeDtypeStruct((d // PLANE_WIDTH, n, PLANE_WIDTH), F32),
                   jax.ShapeDtypeStruct((n, TOP_K), jnp.int32), jax.ShapeDtypeStruct((n, TOP_K), F32)),
        compiler_params=_cparams(("parallel",), 40), name="merge_router")(x, oa, ub, hc, gates, *wts)


def _moe_body(te_ref, tf_ref, tr_ref, xs_ref, wgu_ref, bgu_ref, wd_ref, bd_ref, ys_ref, wgub_ref, wdb_ref):
    del te_ref
    i = pl.program_id(0)
    n_planes, tm, pw = xs_ref.shape

    @pl.when(tf_ref[i] == 1)
    def _():
        wgub_ref[...] = wgu_ref[...].astype(BF16)
        wdb_ref[...] = wd_ref[...].astype(BF16)

    @pl.when(tr_ref[i] > 0)
    def _():
        dff = wd_ref.shape[0]
        live = lax.broadcasted_iota(jnp.int32, (tm, pw), 0) < tr_ref[i]
        gu = bgu_ref[...]
        for p in range(n_planes):
            xp = jnp.where(live, xs_ref[p], 0.0).astype(BF16)
            gu = gu + _dot(xp, wgub_ref[p * pw:(p + 1) * pw, :])
        gate = jnp.minimum(gu[:, :dff], SWIGLU_LIMIT)
        up = jnp.clip(gu[:, dff:], -SWIGLU_LIMIT, SWIGLU_LIMIT)
        hid = (up + 1.0) * gate * jax.nn.sigmoid(SWIGLU_ALPHA * gate)
        out = _dot(hid.astype(BF16), wdb_ref[...]) + bd_ref[...]
        for p in range(n_planes):
            ys_ref[p] = out[:, p * pw:(p + 1) * pw]

    @pl.when(tr_ref[i] == 0)
    def _():
        ys_ref[...] = jnp.zeros_like(ys_ref)


def _moe_experts(layer, tile_expert, tile_first, tile_rows, xs, w_gate_up, b_gate_up, w_down, b_down):
    n_planes, r, pw = xs.shape
    d = n_planes * pw
    tm = MOE_TILE
    n_exp, _, dgu = w_gate_up.shape[1:]
    dff = w_down.shape[2]
    bgu = b_gate_up.reshape(-1, 1, dgu)
    bd = b_down.reshape(-1, 1, d)
    grid_spec = pltpu.PrefetchScalarGridSpec(
        num_scalar_prefetch=3, grid=(r // tm,),
        in_specs=[pl.BlockSpec((n_planes, tm, pw), lambda i, te, tf, tr: (0, i, 0)),
                  pl.BlockSpec((None, None, d, dgu), lambda i, te, tf, tr: (layer, te[i], 0, 0)),
                  pl.BlockSpec((None, 1, dgu), lambda i, te, tf, tr: (layer * n_exp + te[i], 0, 0)),
                  pl.BlockSpec((None, None, dff, d), lambda i, te, tf, tr: (layer, te[i], 0, 0)),
                  pl.BlockSpec((None, 1, d), lambda i, te, tf, tr: (layer * n_exp + te[i], 0, 0))],
        out_specs=pl.BlockSpec((n_planes, tm, pw), lambda i, te, tf, tr: (0, i, 0)),
        scratch_shapes=[pltpu.VMEM((d, dgu), BF16), pltpu.VMEM((dff, d), BF16)])
    return pl.pallas_call(
        _moe_body, grid_spec=grid_spec, out_shape=jax.ShapeDtypeStruct((n_planes, r, pw), F32),
        compiler_params=_cparams(("arbitrary",), 56), name="moe_experts")(
            tile_expert, tile_first, tile_rows, xs, w_gate_up, bgu, w_down, bd)


def _combine_body(x1p_ref, yg_ref, tw_ref, g_ref, b_ref, o_ref, *, alpha):
    tw = tw_ref[...]
    cols = []
    for p in range(x1p_ref.shape[0]):
        acc = alpha * x1p_ref[p]
        for k in range(TOP_K):
            acc = acc + tw[:, k:k + 1] * yg_ref[k, p]
        cols.append(acc)
    o_ref[...] = _layer_norm(jnp.concatenate(cols, axis=1), g_ref[...], b_ref[...])


def _combine(x1p, yg, tw, g, b, alpha):
    n_planes, n, pw = x1p.shape
    d = n_planes * pw
    tm = TOKEN_TILE
    return pl.pallas_call(
        functools.partial(_combine_body, alpha=alpha), grid=(n // tm,),
        in_specs=[pl.BlockSpec((n_planes, tm, pw), lambda i: (0, i, 0)),
                  pl.BlockSpec((TOP_K, n_planes, tm, pw), lambda i: (0, 0, i, 0)),
                  pl.BlockSpec((tm, TOP_K), lambda i: (i, 0)), _const_spec(g.shape), _const_spec(b.shape)],
        out_specs=pl.BlockSpec((tm, d), lambda i: (i, 0)), out_shape=jax.ShapeDtypeStruct((n, d), F32),
        compiler_params=_cparams(("parallel",), 32), name="moe_combine_norm")(x1p, yg, tw, g, b)


def _dispatch_body(idx_ref, pos_ref, meta_ref, cnt_ref, base_ref):
    ph = pl.program_id(0)
    i = pl.program_id(1)
    tb = idx_ref.shape[0]
    lane = lax.broadcasted_iota(jnp.int32, (tb, LANES), 1)
    idx = idx_ref[...]
    onehot = [(idx[:, k:k + 1] == lane).astype(F32) for k in range(TOP_K)]
    colsum = [jnp.sum(o, axis=0, keepdims=True) for o in onehot]

    @pl.when(jnp.logical_and(ph == 0, i == 0))
    def _():
        cnt_ref[...] = jnp.zeros_like(cnt_ref)

    @pl.when(ph == 0)
    def _():
        cnt_ref[...] = cnt_ref[...] + (colsum[0] + colsum[1] + colsum[2] + colsum[3])

    @pl.when(jnp.logical_and(ph == 1, i == 0))
    def _():
        cnt = cnt_ref[...]
        ptiles = jnp.floor((cnt + (MOE_TILE - 1)) * (1.0 / MOE_TILE))
        tile_end = _lane_cumsum(ptiles)
        base_ref[...] = (tile_end - ptiles) * MOE_TILE
        row = lax.broadcasted_iota(jnp.int32, meta_ref.shape, 0)
        meta_ref[...] = jnp.where(row == 0, tile_end[0:1, :], jnp.where(row == 1, cnt[0:1, :], 0.0)).astype(jnp.int32)

    @pl.when(ph == 1)
    def _():
        earlier = (lax.broadcasted_iota(jnp.int32, (tb, tb), 0) > lax.broadcasted_iota(jnp.int32, (tb, tb), 1))
        earlier = earlier.astype(BF16)
        base = base_ref[0:1, :]
        slot = lax.broadcasted_iota(jnp.int32, (tb, TOP_K), 1)
        pos = jnp.zeros((tb, TOP_K), F32)
        for k in range(TOP_K):
            seen = _dot(earlier, onehot[k].astype(BF16)) + base
            pos = jnp.where(slot == k, jnp.sum(onehot[k] * seen, axis=1, keepdims=True), pos)
            base = base + colsum[k]
        base_ref[...] = jnp.broadcast_to(base, base_ref.shape)
        pos_ref[...] = pos.astype(jnp.int32)


def _dispatch(top_idx):
    n = top_idx.shape[0]
    tm = MOE_TILE
    n_tiles = -(-(n * TOP_K) // tm) + N_EXPERTS
    tb = next(t for t in DISPATCH_TILES if n % t == 0)
    pos, meta = pl.pallas_call(
        _dispatch_body, grid=(2, n // tb),
        in_specs=[pl.BlockSpec((tb, TOP_K), lambda ph, i: (i, 0))],
        out_specs=(pl.BlockSpec((tb, TOP_K), lambda ph, i: (i * ph, 0)), _const_spec((SUBLANES, LANES))),
        out_shape=(jax.ShapeDtypeStruct((n, TOP_K), jnp.int32), jax.ShapeDtypeStruct((SUBLANES, LANES), jnp.int32)),
        scratch_shapes=[pltpu.VMEM((SUBLANES, LANES), F32), pltpu.VMEM((SUBLANES, LANES), F32)],
        compiler_params=_cparams(("arbitrary", "arbitrary"), 32), name="moe_dispatch")(top_idx)
    tile_end, cnt = meta[0, :N_EXPERTS], meta[1, :N_EXPERTS]
    tiles = jnp.arange(n_tiles, dtype=jnp.int32)
    te = jnp.minimum(jnp.sum((tiles[:, None] >= tile_end[None, :]).astype(jnp.int32), axis=1), N_EXPERTS - 1)
    tile_start = tile_end - (cnt + tm - 1) // tm
    tr = jnp.clip(cnt[te] - (tiles - tile_start[te]) * tm, 0, tm) * (tiles < tile_end[-1]).astype(jnp.int32)
    tf = jnp.concatenate([jnp.ones((1,), jnp.int32), (te[1:] != te[:-1]).astype(jnp.int32)])
    return pos, te, tf, tr, n_tiles * tm


def _sc_mesh():
    return plsc.VectorSubcoreMesh(core_axis_name="c", subcore_axis_name="s")


def _sc_gather_rows(table, idx):
    m = idx.shape[0]
    w = table.shape[1]

    @functools.partial(pl.kernel, out_type=jax.ShapeDtypeStruct((m, w), table.dtype), mesh=_sc_mesh())
    def gather(t_hbm, i_hbm, o_hbm):
        def body(i_vmem, o_vmem):
            pltpu.sync_copy(t_hbm.at[i_vmem.at[0]], o_vmem)

        pltpu.emit_pipeline(
            body, grid=(m // SC_WINDOW,),
            in_specs=[pl.BlockSpec((1, SC_WINDOW), index_map=lambda i: (0, i))],
            out_specs=[pl.BlockSpec((SC_WINDOW, w), index_map=lambda i: (i, 0))],
            core_axis_name=("c", "s"), dimension_semantics=(pltpu.PARALLEL,))(i_hbm, o_hbm)

    return gather(table, idx.reshape(1, m))


def _sc_scatter_rows(rows, idx, n_out):
    n_src, w = rows.shape
    m = idx.shape[0]
    src_blocks = n_src // SC_WINDOW

    @functools.partial(pl.kernel, out_type=jax.ShapeDtypeStruct((n_out, w), rows.dtype), mesh=_sc_mesh(),
                       scratch_types=[])
    def scatter(x_hbm, i_hbm, o_hbm):
        def body(x_vmem, i_vmem):
            pltpu.sync_copy(x_vmem, o_hbm.at[i_vmem.at[0]])

        pltpu.emit_pipeline(
            body, grid=(m // SC_WINDOW,),
            in_specs=[pl.BlockSpec((SC_WINDOW, w), index_map=lambda i: (lax.rem(i, src_blocks), 0)),
                      pl.BlockSpec((1, SC_WINDOW), index_map=lambda i: (0, i))],
            out_specs=[], core_axis_name=("c", "s"), dimension_semantics=(pltpu.PARALLEL,))(x_hbm, i_hbm)

    return scatter(rows, idx.reshape(1, m))


def _plane_rows(pos, n_planes, rows_per_plane):
    off = jnp.arange(n_planes, dtype=jnp.int32) * rows_per_plane
    return (pos.T[:, None, :] + off[None, :, None]).reshape(-1)


def _pad_rows(a, rows):
    return jnp.pad(a, ((0, 0), (rows - a.shape[1], 0), (0, 0)))


def _block_diag(w):
    h, bd, _ = w.shape
    eye = jnp.eye(h, dtype=w.dtype)
    return jnp.einsum("hij,hg->higj", w, eye).reshape(h * bd, h * bd)


def kernel(x_prompt, x_sample, cache_k, cache_v, cache_logf, state_conv_b, state_conv_c, state_h, page_table,
           w_in, b_f, w_att_o, conv_b_w, conv_b_b, lnb_g, lnb_b, w_conv_o, b_conv_o, conv_c_w, conv_c_b,
           w_rg_a, b_rg_a, w_rg_x, b_rg_x, rg_lambda, w_rg_o, w_out, ln1_g, ln1_b, w_router, b_router,
           w_gate_up, b_gate_up, w_down, b_down, ln2_g, ln2_b):
    bp, seq, d = x_prompt.shape
    bs, ns, _ = x_sample.shape
    depth, n_pool, page = cache_k.shape[:3]
    w_conv = state_conv_b.shape[-1]
    w_rg = state_conv_c.shape[-1]
    alpha = (2 * depth) ** 0.25
    n_p, n_s = bp * seq, bs * ns
    off_f = 3 * W_ATT
    off_glu = off_f + H_ATT
    off_rg = off_glu + 2 * w_conv
    off_gate = off_rg + 2 * w_rg

    ckt = cache_k.transpose(0, 1, 3, 4, 2).reshape(depth, n_pool, W_ATT, page)
    cvt = cache_v.transpose(0, 1, 3, 4, 2).reshape(depth, n_pool, W_ATT, page)
    cft = cache_logf.transpose(0, 1, 3, 2)
    head_eye = jnp.eye(H_ATT, dtype=BF16)

    xp = x_prompt.reshape(n_p, d)
    xs = x_sample.reshape(n_s, d)
    outs = {k: [] for k in ("kp", "vp", "fp", "ks", "vs", "fs", "cbp", "cbs", "ccp", "ccs", "hp", "hs")}

    for l in range(depth):
        wl = w_in[l]
        wf = wl[:, off_f:off_glu]
        in_w = (wl[:, :2 * W_ATT].astype(BF16), wl[:, :off_f].T.astype(BF16),
                jnp.pad(wf, ((0, 0), (0, LANES - H_ATT))).astype(BF16), wf.T.astype(BF16),
                b_f[l].reshape(1, H_ATT), b_f[l].reshape(H_ATT, 1),
                wl[:, off_glu:off_rg].astype(BF16), wl[:, off_rg:off_gate].astype(BF16),
                wl[:, off_gate:].astype(BF16))
        cbw = jnp.broadcast_to(conv_b_w[l][:, None, :], (CONV_WIDTH, SUBLANES, w_conv))
        ccw = jnp.pad(conv_c_w[l], ((0, CONVC_HALO - RG_CONV_WIDTH), (0, 0)))
        rg_w = (ccw, conv_c_b[l].reshape(1, -1), _block_diag(w_rg_a[l]).astype(BF16), b_rg_a[l].reshape(1, -1),
                _block_diag(w_rg_x[l]).astype(BF16), b_rg_x[l].reshape(1, -1), rg_lambda[l].reshape(1, -1))
        cb_w = (cbw, conv_b_b[l].reshape(1, -1), lnb_g[l].reshape(1, -1), lnb_b[l].reshape(1, -1))
        wr = jnp.pad(w_router[l], ((0, 0), (0, LANES - N_EXPERTS)))
        br = jnp.pad(b_router[l], (0, LANES - N_EXPERTS), constant_values=NEG_BIG).reshape(1, LANES)
        mg_w = (w_att_o[l].astype(BF16), w_conv_o[l].astype(BF16), b_conv_o[l].reshape(1, -1),
                w_rg_o[l].astype(BF16), w_out[l].astype(BF16), ln1_g[l].reshape(1, -1), ln1_b[l].reshape(1, -1),
                wr.astype(BF16), (wr - wr.astype(BF16).astype(F32)).astype(BF16), br)

        (q, kb, qtm, kt, vt, vtb, lf, lft, u, rgx, gg, gates) = _inproj(xp, in_w, seq)
        oa = _fox_prompt(qtm, kb, vtb, _cumsum(lf, seq), bp, seq)
        u3 = u.reshape(bp, seq, w_conv)
        ub = _convb(u3, jnp.zeros((bp, CONVB_HALO, w_conv), F32), *cb_w)
        rgx3 = rgx.reshape(bp, seq, w_rg)
        hc, hl = _rglru(rgx3, gg.reshape(bp, seq, w_rg), jnp.zeros((bp, CONVC_HALO, w_rg), F32),
                        jnp.zeros((bp, 1, w_rg), F32), *rg_w)
        x1p, tip, twp = _merge(xp, oa, ub.reshape(n_p, w_conv), hc.reshape(n_p, w_rg), gates, mg_w, alpha)
        outs["kp"].append(kt.reshape(bp, H_ATT, HD_ATT, seq).transpose(0, 3, 1, 2))
        outs["vp"].append(vt.reshape(bp, H_ATT, HD_ATT, seq).transpose(0, 3, 1, 2))
        outs["fp"].append(lft.transpose(0, 2, 1))
        outs["cbp"].append(u3[:, seq - (CONV_WIDTH - 1):])
        outs["ccp"].append(rgx3[:, seq - (RG_CONV_WIDTH - 1):])
        outs["hp"].append(hl.reshape(bp, w_rg))

        (q, kb, qtm, kt, vt, vtb, lf, lft, u, rgx, gg, gates) = _inproj(xs, in_w, n_s)
        qbd = jnp.einsum("bthd,hg->bhtgd", q.reshape(bs, ns, H_ATT, HD_ATT), head_eye)
        qbd = qbd.reshape(bs, H_ATT * ns, W_ATT)

        def new_page(a):
            a = a.reshape(a.shape[0], bs, ns).transpose(1, 0, 2)
            return jnp.pad(a, ((0, 0), (0, 0), (0, page - ns)))

        oa = _fox_sample(l, page_table, qbd, ckt, cvt, cft, new_page(kt[0]), new_page(vt[0]), new_page(lft[0]), ns)
        u3 = u.reshape(bs, ns, w_conv)
        ub = _convb(u3, _pad_rows(state_conv_b[l], CONVB_HALO), *cb_w)
        rgx3 = rgx.reshape(bs, ns, w_rg)
        hc, hl = _rglru(rgx3, gg.reshape(bs, ns, w_rg), _pad_rows(state_conv_c[l], CONVC_HALO),
                        state_h[l].reshape(bs, 1, w_rg), *rg_w)
        x1s, tis, tws = _merge(xs, oa.reshape(n_s, W_ATT), ub.reshape(n_s, w_conv), hc.reshape(n_s, w_rg),
                                     gates, mg_w, alpha)
        outs["ks"].append(kt[0].T.reshape(bs, ns, H_ATT, HD_ATT))
        outs["vs"].append(vt[0].T.reshape(bs, ns, H_ATT, HD_ATT))
        outs["fs"].append(lf.reshape(bs, ns, H_ATT))
        outs["cbs"].append(jnp.concatenate([state_conv_b[l], u3], axis=1)[:, ns:])
        outs["ccs"].append(jnp.concatenate([state_conv_c[l], rgx3], axis=1)[:, ns:])
        outs["hs"].append(hl.reshape(bs, w_rg))

        pos, te, tf, tr, n_rows = _dispatch(jnp.concatenate([tip, tis], axis=0))
        x1_all = jnp.concatenate([x1p, x1s], axis=1)
        n_planes, n_all, pw = x1_all.shape
        xs_rows = _sc_scatter_rows(x1_all.reshape(n_planes * n_all, pw), _plane_rows(pos, n_planes, n_rows),
                                   n_planes * n_rows)
        ys = _moe_experts(l, te, tf, tr, xs_rows.reshape(n_planes, n_rows, pw), w_gate_up, b_gate_up, w_down, b_down)
        ys = ys.reshape(n_planes * n_rows, pw)
        g2, b2 = ln2_g[l].reshape(1, -1), ln2_b[l].reshape(1, -1)
        yg = _sc_gather_rows(ys, _plane_rows(pos[:n_p], n_planes, n_rows)).reshape(TOP_K, n_planes, n_p, pw)
        xp = _combine(x1p, yg, twp, g2, b2, alpha)
        yg = _sc_gather_rows(ys, _plane_rows(pos[n_p:], n_planes, n_rows)).reshape(TOP_K, n_planes, n_s, pw)
        xs = _combine(x1s, yg, tws, g2, b2, alpha)

    st = {k: jnp.stack(v, 0) for k, v in outs.items()}
    return (xp.reshape(bp, seq, d), xs.reshape(bs, ns, d), st["kp"], st["vp"], st["fp"], st["ks"], st["vs"],
            st["fs"], st["cbp"], st["cbs"], st["ccp"], st["ccs"], st["hp"], st["hs"])
```

```python
import functools
import math

import jax
import jax.numpy as jnp
from jax import lax
from jax.experimental import pallas as pl
from jax.experimental.pallas import tpu as pltpu
from jax.experimental.pallas import tpu_sc as plsc

F32 = jnp.float32
BF16 = jnp.bfloat16

H_ATT = 8
HD_ATT = 64
W_ATT = H_ATT * HD_ATT
CONV_WIDTH = 31
RG_CONV_WIDTH = 4
H_RG = 8
RG_C = 8.0
N_EXPERTS = 32
TOP_K = 4
SWIGLU_LIMIT = 7.0
SWIGLU_ALPHA = 1.702
LN_EPS = 1e-5
N_BRANCH = 3

LANES = 128
SUBLANES = 8
NEG_BIG = -1e30

TOKEN_TILE = 256
MERGE_TILE = 512
ATT_TILE = 512
CUMSUM_TILE = 512
SEQ_TILE = 256
CONVB_HALO = 32
CONVC_HALO = 8
CONV_CHUNK = 32
PAGES_PER_STEP = 16
MOE_TILE = 512
PLANE_WIDTH = 256
SC_WINDOW = 128
DISPATCH_TILES = (1280, 1024, 768, 512, 256)


def _cparams(sem, vmem_mb):
    return pltpu.CompilerParams(dimension_semantics=sem, vmem_limit_bytes=vmem_mb << 20)


def _const_spec(shape):
    nd = len(shape)
    return pl.BlockSpec(shape, lambda *_: (0,) * nd)


def _log_sigmoid(z):
    return jnp.minimum(z, 0.0) - jnp.log1p(jnp.exp(-jnp.abs(z)))


def _softplus(z):
    return jnp.maximum(z, 0.0) + jnp.log1p(jnp.exp(-jnp.abs(z)))


def _layer_norm(x, g, b):
    mu = jnp.mean(x, axis=-1, keepdims=True)
    xc = x - mu
    var = jnp.mean(xc * xc, axis=-1, keepdims=True)
    return xc * lax.rsqrt(var + LN_EPS) * g + b


def _pack_halves(x):
    n = x.shape[1] // 2
    lo = lax.bitcast_convert_type(x[:, :n].astype(BF16).astype(F32), jnp.uint32)
    hi = lax.bitcast_convert_type(x[:, n:].astype(BF16).astype(F32), jnp.uint32)
    return lax.bitcast_convert_type(hi | (lo >> 16), jnp.int32)


def _unpack_halves(w):
    u = lax.bitcast_convert_type(w, jnp.uint32)
    return (lax.bitcast_convert_type(u << 16, F32),
            lax.bitcast_convert_type(u & jnp.uint32(0xFFFF0000), F32))


def _dot(a, b):
    return jnp.dot(a, b, preferred_element_type=F32)


def _dot_nt(a, b, precision=None):
    return lax.dot_general(a, b, (((1,), (1,)), ((), ())), preferred_element_type=F32, precision=precision)


def _store_slab(ref, layer, val):
    if len(ref.shape) == val.ndim:
        ref[...] = val
    else:
        for l in range(ref.shape[0]):
            ref[l] = val if l == layer else jnp.zeros_like(val)


def _inproj_body(x_ref, wqk_ref, wt_ref, wf_ref, wft_ref, bf_ref, bft_ref, wglu_ref, wrg_ref, wgate_ref,
                 q_ref, kb_ref, qtm_ref, kt_ref, vt_ref, vtb_ref, lf_ref, lft_ref, u_ref, rgx_ref, gg_ref, gates_ref,
                 *, layer):
    xb = x_ref[...].astype(BF16)
    w = W_ATT
    scale = HD_ATT ** -0.5
    qk = _dot(xb, wqk_ref[...])
    q_ref[...] = (qk[:, :w] * scale).astype(BF16)
    kb_ref[...] = qk[:, w:].astype(BF16)
    t = _dot_nt(wt_ref[...], xb)
    _store_slab(kt_ref, layer, t[w:2 * w, :])
    vt = t[2 * w:, :]
    _store_slab(vt_ref, layer, vt)
    vtb_ref[...] = vt.astype(BF16)
    qt = (t[:w, :] * scale).astype(BF16)
    pair = 2 * HD_ATT
    lower = lax.broadcasted_iota(jnp.int32, (pair, qt.shape[1]), 0) < HD_ATT
    for h in range(H_ATT):
        slab = qt[(h // 2) * pair:(h // 2 + 1) * pair, :]
        qtm_ref[h] = jnp.where(lower if h % 2 == 0 else jnp.logical_not(lower), slab, jnp.zeros_like(slab))
    zf = _dot(xb, wf_ref[...])
    lf_ref[...] = _log_sigmoid(zf[:, :H_ATT] + bf_ref[...])
    _store_slab(lft_ref, layer, _log_sigmoid(_dot_nt(wft_ref[...], xb) + bft_ref[...]))
    glu = _dot(xb, wglu_ref[...])
    c = glu.shape[1] // 2
    u_ref[...] = glu[:, :c] * jax.nn.sigmoid(glu[:, c:])
    rg = _dot(xb, wrg_ref[...])
    c = rg.shape[1] // 2
    rgx_ref[...] = rg[:, :c]
    gg_ref[...] = jax.nn.gelu(rg[:, c:]).astype(gg_ref.dtype)
    d = gates_ref.shape[1] // N_BRANCH
    for j in range(N_BRANCH):
        gate = jax.nn.sigmoid(_dot(xb, wgate_ref[:, j * d:(j + 1) * d]))
        gates_ref[:, j * d:(j + 1) * d] = gate.astype(gates_ref.dtype)


_INPROJ_STACKED_OUTS = (3, 4, 7)


def _inproj_carry_body(*refs, n_in, layer):
    _inproj_body(*refs[:n_in], *refs[n_in + len(_INPROJ_STACKED_OUTS):], layer=layer)


def _inproj(x, wts, seq, layer=0, depth=1, carried=None):
    n, d = x.shape
    tm = TOKEN_TILE
    wglu, wrg, wgate = wts[-3:]
    wc, wr = wglu.shape[1] // 2, wrg.shape[1] // 2
    bps = seq // tm

    def row(c):
        return pl.BlockSpec((tm, c), lambda i: (i, 0))

    def col(r):
        return pl.BlockSpec((None, r, tm), lambda i: (i // bps, 0, i % bps))

    def slab(r):
        if carried is None:
            return pl.BlockSpec((depth, None, r, tm), lambda i: (0, i // bps, 0, i % bps))
        return pl.BlockSpec((None, None, r, tm), lambda i: (layer, i // bps, 0, i % bps))

    def slabshape(r):
        return jax.ShapeDtypeStruct((depth, n // seq, r, seq), F32)

    out_shape = (
        jax.ShapeDtypeStruct((n, W_ATT), BF16), jax.ShapeDtypeStruct((n, W_ATT), BF16),
        jax.ShapeDtypeStruct((H_ATT, 2 * HD_ATT, n), BF16),
        slabshape(W_ATT), slabshape(W_ATT), jax.ShapeDtypeStruct((n // seq, W_ATT, seq), BF16),
        jax.ShapeDtypeStruct((n, H_ATT), F32), slabshape(H_ATT), jax.ShapeDtypeStruct((n, wc), F32),
        jax.ShapeDtypeStruct((n, wr), F32), jax.ShapeDtypeStruct((n, wr), BF16),
        jax.ShapeDtypeStruct((n, wgate.shape[1]), BF16))
    out_specs = (row(W_ATT), row(W_ATT), pl.BlockSpec((H_ATT, 2 * HD_ATT, tm), lambda i: (0, 0, i)),
                 slab(W_ATT), slab(W_ATT), col(W_ATT),
                 row(H_ATT), slab(H_ATT), row(wc), row(wr), row(wr), row(wgate.shape[1]))
    in_specs = [row(d)] + [_const_spec(a.shape) for a in wts]
    args = [x, *wts]
    if carried is None:
        body, aliases = functools.partial(_inproj_body, layer=layer), {}
    else:
        body = functools.partial(_inproj_carry_body, n_in=len(args), layer=layer)
        aliases = {len(args) + j: o for j, o in enumerate(_INPROJ_STACKED_OUTS)}
        in_specs += [pl.BlockSpec(memory_space=pl.ANY)] * len(carried)
        args += list(carried)
    return pl.pallas_call(
        body, grid=(n // tm,), in_specs=in_specs, out_specs=out_specs, out_shape=out_shape,
        input_output_aliases=aliases,
        compiler_params=_cparams(("parallel",), 56), name="inproj")(*args)


def _lane_cumsum(c):
    n = c.shape[1]
    lane = lax.broadcasted_iota(jnp.int32, c.shape, 1)
    d = 1
    while d < n:
        c = c + jnp.where(lane >= d, pltpu.roll(c, d, axis=1), 0.0)
        d *= 2
    return c


def _cumsum_body(f_ref, c_ref, carry_ref, *, blocks_per_seq):
    @pl.when(pl.program_id(0) % blocks_per_seq == 0)
    def _():
        carry_ref[...] = jnp.zeros_like(carry_ref)

    c = f_ref[...]
    n = c.shape[0]
    row = lax.broadcasted_iota(jnp.int32, c.shape, 0)
    d = 1
    while d < n:
        c = c + jnp.where(row >= d, pltpu.roll(c, d, axis=0), 0.0)
        d *= 2
    c = c + carry_ref[0:1, :]
    carry_ref[...] = jnp.broadcast_to(c[n - 1:n, :], carry_ref.shape)
    for h in range(H_ATT):
        c_ref[h] = jnp.broadcast_to(c[:, h:h + 1], (n, LANES))


def _cumsum(lf, seq):
    n, h = lf.shape
    tc = min(CUMSUM_TILE, seq)
    return pl.pallas_call(
        functools.partial(_cumsum_body, blocks_per_seq=seq // tc), grid=(n // tc,),
        in_specs=[pl.BlockSpec((tc, h), lambda i: (i, 0))],
        out_specs=pl.BlockSpec((h, tc, LANES), lambda i: (0, i, 0)),
        out_shape=jax.ShapeDtypeStruct((h, n, LANES), F32),
        scratch_shapes=[pltpu.VMEM((SUBLANES, h), F32)],
        compiler_params=_cparams(("arbitrary",), 16), name="logf_cumsum")(lf)


def _fox_prompt_body(qi_ref, ki_ref, qtm_ref, k_ref, vt_ref, ck_ref, cq_ref, o_ref, m_ref, l_ref, acc_ref):
    qi = qi_ref[pl.program_id(1)]
    ki = ki_ref[pl.program_id(1)]
    tk, tq = k_ref.shape[0], qtm_ref.shape[2]
    pair = 2 * HD_ATT

    @pl.when(ki == 0)
    def _():
        m_ref[...] = jnp.full_like(m_ref, NEG_BIG)
        l_ref[...] = jnp.zeros_like(l_ref)
        acc_ref[...] = jnp.zeros_like(acc_ref)

    def update(diagonal):
        if diagonal:
            keep = lax.broadcasted_iota(jnp.int32, (tk, tq), 0) <= lax.broadcasted_iota(jnp.int32, (tk, tq), 1)
        for h in range(H_ATT):
            s = _dot(k_ref[:, (h // 2) * pair:(h // 2 + 1) * pair], qtm_ref[h])
            bias = ck_ref[h] - cq_ref[h, 0:1, :]
            s = s - jnp.concatenate([bias] * (tq // LANES), axis=1)
            if diagonal:
                s = jnp.where(keep, s, NEG_BIG)
            m_old = m_ref[h]
            m_new = jnp.maximum(m_old, jnp.max(s, axis=0, keepdims=True))
            p = jnp.exp(s - m_new)
            alpha = jnp.exp(m_old - m_new)
            l_ref[h] = alpha * l_ref[h] + jnp.sum(p, axis=0, keepdims=True)
            acc_ref[h] = alpha * acc_ref[h] + _dot(vt_ref[h * HD_ATT:(h + 1) * HD_ATT, :], p.astype(BF16))
            m_ref[h] = m_new

    @pl.when(ki < qi)
    def _():
        update(False)

    @pl.when(ki == qi)
    def _():
        update(True)
        for j in range(H_ATT // 2):
            o2 = jnp.concatenate([acc_ref[2 * j] / l_ref[2 * j], acc_ref[2 * j + 1] / l_ref[2 * j + 1]], axis=0)
            o_ref[:, j * pair:(j + 1) * pair] = o2.T.astype(o_ref.dtype)


def _fox_prompt(qtm, kb, vt, cb, batch, seq):
    n = kb.shape[0]
    t = min(ATT_TILE, seq)
    nb = seq // t
    pairs = [(qi, ki) for qi in range(nb) for ki in range(qi + 1)]
    qi_of = jnp.array([p[0] for p in pairs], jnp.int32)
    ki_of = jnp.array([p[1] for p in pairs], jnp.int32)
    grid_spec = pltpu.PrefetchScalarGridSpec(
        num_scalar_prefetch=2, grid=(batch, len(pairs)),
        in_specs=[pl.BlockSpec((H_ATT, 2 * HD_ATT, t), lambda b, p, qs, ks: (0, 0, b * nb + qs[p])),
                  pl.BlockSpec((t, W_ATT), lambda b, p, qs, ks: (b * nb + ks[p], 0)),
                  pl.BlockSpec((None, W_ATT, t), lambda b, p, qs, ks: (b, 0, ks[p])),
                  pl.BlockSpec((H_ATT, t, LANES), lambda b, p, qs, ks: (0, b * nb + ks[p], 0)),
                  pl.BlockSpec((H_ATT, SUBLANES, LANES),
                               lambda b, p, qs, ks: (0, (b * nb + qs[p]) * (t // SUBLANES), 0))],
        out_specs=pl.BlockSpec((t, W_ATT), lambda b, p, qs, ks: (b * nb + qs[p], 0)),
        scratch_shapes=[pltpu.VMEM((H_ATT, 1, t), F32), pltpu.VMEM((H_ATT, 1, t), F32),
                        pltpu.VMEM((H_ATT, HD_ATT, t), F32)])
    return pl.pallas_call(
        _fox_prompt_body, grid_spec=grid_spec, out_shape=jax.ShapeDtypeStruct((n, W_ATT), BF16),
        compiler_params=_cparams(("parallel", "arbitrary"), 40), name="fox_prompt")(
            qi_of, ki_of, qtm, kb, vt, cb, cb)


def _fox_sample_body(pt_ref, qbd_ref, *refs, n_pages_step, n_new):
    del pt_ref
    np_ = n_pages_step
    k_refs = refs[:np_]
    v_refs = refs[np_:2 * np_]
    f_refs = refs[2 * np_:3 * np_]
    kn_ref, vn_ref, fn_ref, o_ref, m_ref, l_ref, acc_ref, carry_ref, kb_ref, vb_ref = refs[3 * np_:]
    c = pl.program_id(1)
    rows = qbd_ref.shape[0]
    page = kn_ref.shape[1]

    @pl.when(c == 0)
    def _():
        m_ref[...] = jnp.full_like(m_ref, NEG_BIG)
        l_ref[...] = jnp.zeros_like(l_ref)
        acc_ref[...] = jnp.zeros_like(acc_ref)
        carry_ref[...] = jnp.zeros_like(carry_ref)

    def decay_rows(ft):
        ct = _lane_cumsum(ft) + carry_ref[:, 0:1]
        carry_ref[...] = jnp.broadcast_to(ct[:, ct.shape[1] - 1:], carry_ref.shape)
        return jnp.concatenate([jnp.broadcast_to(ct[h:h + 1, :], (n_new, ct.shape[1])) for h in range(H_ATT)], axis=0)

    def update(kb, vb, cexp, keep):
        s = _dot(qbd_ref[...], kb) - cexp
        if keep is not None:
            s = jnp.where(keep, s, NEG_BIG)
        m_old = m_ref[...]
        m_new = jnp.maximum(m_old, jnp.max(s, axis=-1, keepdims=True))
        p = jnp.exp(s - m_new)
        alpha = jnp.exp(m_old - m_new)
        l_ref[...] = alpha * l_ref[...] + jnp.sum(p, axis=-1, keepdims=True)
        acc_ref[...] = alpha * acc_ref[...] + _dot_nt(p.astype(BF16), vb)
        m_ref[...] = m_new

    for j in range(np_):
        kb_ref[:, j * page:(j + 1) * page] = k_refs[j][...].astype(BF16)
        vb_ref[:, j * page:(j + 1) * page] = v_refs[j][...].astype(BF16)
    ft = jnp.concatenate([r[...] for r in f_refs], axis=1)
    update(kb_ref[...], vb_ref[...], decay_rows(ft), None)

    @pl.when(c == pl.num_programs(1) - 1)
    def _():
        tok = lax.broadcasted_iota(jnp.int32, (rows, page), 0) % n_new
        key = lax.broadcasted_iota(jnp.int32, (rows, page), 1)
        update(kn_ref[...].astype(BF16), vn_ref[...].astype(BF16), decay_rows(fn_ref[...]), key <= tok)
        acc = acc_ref[...] / l_ref[...]
        for h in range(H_ATT):
            o_ref[:, h * HD_ATT:(h + 1) * HD_ATT] = acc[h * n_new:(h + 1) * n_new,
                                                        h * HD_ATT:(h + 1) * HD_ATT].astype(o_ref.dtype)


def _fox_sample(layer, page_table, qbd, cache_kt, cache_vt, cache_ft, kt_new, vt_new, ft_new, n_new):
    bsz, n_pages = page_table.shape
    page = cache_kt.shape[3]
    np_ = PAGES_PER_STEP
    while n_pages % np_:
        np_ //= 2
    rows = qbd.shape[1]

    def page_spec(r, j):
        return pl.BlockSpec((None, None, r, page), lambda b, c, pt: (layer, pt[b, c * np_ + j], 0, 0))

    def seq_spec(r, w):
        return pl.BlockSpec((None, r, w), lambda b, c, pt: (b, 0, 0))

    in_specs = ([seq_spec(rows, W_ATT)]
                + [page_spec(W_ATT, j) for j in range(np_)] + [page_spec(W_ATT, j) for j in range(np_)]
                + [page_spec(H_ATT, j) for j in range(np_)]
                + [seq_spec(W_ATT, page), seq_spec(W_ATT, page), seq_spec(H_ATT, page)])
    grid_spec = pltpu.PrefetchScalarGridSpec(
        num_scalar_prefetch=1, grid=(bsz, n_pages // np_), in_specs=in_specs,
        out_specs=seq_spec(n_new, W_ATT),
        scratch_shapes=[pltpu.VMEM((rows, 1), F32), pltpu.VMEM((rows, 1), F32), pltpu.VMEM((rows, W_ATT), F32),
                        pltpu.VMEM((H_ATT, LANES), F32),
                        pltpu.VMEM((W_ATT, np_ * page), BF16), pltpu.VMEM((W_ATT, np_ * page), BF16)])
    return pl.pallas_call(
        functools.partial(_fox_sample_body, n_pages_step=np_, n_new=n_new),
        grid_spec=grid_spec, out_shape=jax.ShapeDtypeStruct((bsz, n_new, W_ATT), BF16),
        compiler_params=_cparams(("parallel", "arbitrary"), 40), name="fox_sample")(
            page_table, qbd, *([cache_kt] * np_), *([cache_vt] * np_), *([cache_ft] * np_), kt_new, vt_new, ft_new)


def _convb_body(*refs, tm, has_prev):
    if has_prev:
        u_ref, prev_ref, hist0_ref, w_ref, b_ref, g_ref, beta_ref, o_ref, ext_ref = refs
        hist = jnp.where(pl.program_id(1) == 0, hist0_ref[...], prev_ref[...])
    else:
        u_ref, hist0_ref, w_ref, b_ref, g_ref, beta_ref, o_ref, ext_ref = refs
        hist = hist0_ref[...]
    halo = CONVB_HALO
    ext_ref[0:halo, :] = hist
    ext_ref[halo:halo + tm, :] = u_ref[...]
    lead = halo - (CONV_WIDTH - 1)
    ch = min(CONV_CHUNK, tm)
    for r0 in range(0, tm, ch):
        acc = jnp.broadcast_to(b_ref[...], (ch, b_ref.shape[1]))
        for r in range(SUBLANES):
            taps = [k for k in range(CONV_WIDTH) if (k + lead) % SUBLANES == r]
            if not taps:
                continue
            win = ext_ref[r0 + r:r0 + taps[-1] + lead + ch, :]
            for k in taps:
                off = k + lead - r
                wk = jnp.concatenate([w_ref[k]] * (ch // SUBLANES), axis=0)
                acc = acc + wk * win[off:off + ch, :]
        y = _layer_norm(acc, g_ref[...], beta_ref[...])
        o_ref[r0:r0 + ch, :] = (y * jax.nn.sigmoid(y)).astype(o_ref.dtype)


def _convb(u, hist0, w, b, g, beta):
    bsz, t, c = u.shape
    tm = min(SEQ_TILE, t)
    nt = t // tm
    has_prev = nt > 1
    halo = CONVB_HALO
    r = tm // halo if has_prev else 1
    in_specs = [pl.BlockSpec((None, tm, c), lambda bi, i: (bi, i, 0))]
    args = [u]
    if has_prev:
        in_specs.append(pl.BlockSpec((None, halo, c), lambda bi, i: (bi, jnp.maximum(i * r - 1, 0), 0)))
        args.append(u)
    in_specs += [pl.BlockSpec((None, halo, c), lambda bi, i: (bi, 0, 0)),
                 _const_spec(w.shape), _const_spec(b.shape), _const_spec(g.shape), _const_spec(beta.shape)]
    args += [hist0, w, b, g, beta]
    return pl.pallas_call(
        functools.partial(_convb_body, tm=tm, has_prev=has_prev), grid=(bsz, nt),
        in_specs=in_specs, out_specs=pl.BlockSpec((None, tm, c), lambda bi, i: (bi, i, 0)),
        out_shape=jax.ShapeDtypeStruct((bsz, t, c), BF16),
        scratch_shapes=[pltpu.VMEM((halo + tm, c), F32)],
        compiler_params=_cparams(("parallel", "arbitrary"), 24), name="conformer_conv")(*args)


def _rglru_body(*refs, tm, has_prev):
    if has_prev:
        (x_ref, prev_ref, hist0_ref, gg_ref, h0_ref, w_ref, b_ref, wa_ref, ba_ref, wx_ref, bx_ref, lam_ref,
         o_ref, hl_ref, ext_ref, h_ref) = refs
        hist = jnp.where(pl.program_id(1) == 0, hist0_ref[...], prev_ref[...])
    else:
        (x_ref, hist0_ref, gg_ref, h0_ref, w_ref, b_ref, wa_ref, ba_ref, wx_ref, bx_ref, lam_ref,
         o_ref, hl_ref, ext_ref, h_ref) = refs
        hist = hist0_ref[...]

    @pl.when(pl.program_id(1) == 0)
    def _():
        h_ref[...] = jnp.broadcast_to(h0_ref[...], h_ref.shape)

    halo = CONVC_HALO
    ext_ref[0:halo, :] = hist
    ext_ref[halo:halo + tm, :] = x_ref[...]
    lead = halo - (RG_CONV_WIDTH - 1)
    xc = jnp.broadcast_to(b_ref[...], x_ref.shape)
    for k in range(RG_CONV_WIDTH):
        xc = xc + w_ref[k:k + 1, :] * ext_ref[k + lead:k + lead + tm, :]
    xcb = xc.astype(BF16)
    r = jax.nn.sigmoid(_dot(xcb, wa_ref[...]) + ba_ref[...])
    ig = jax.nn.sigmoid(_dot(xcb, wx_ref[...]) + bx_ref[...])
    log_a = -RG_C * r * _softplus(-lam_ref[...])
    a = jnp.exp(log_a)
    u = jnp.sqrt(-jnp.tanh(log_a) * (1.0 + a * a)) * (ig * xc)
    row = lax.broadcasted_iota(jnp.int32, a.shape, 0)
    d = 1
    while d < tm:
        a_sh = pltpu.roll(a, d, axis=0)
        u_sh = pltpu.roll(u, d, axis=0)
        keep = row >= d
        u = jnp.where(keep, a * u_sh + u, u)
        a = jnp.where(keep, a * a_sh, a)
        d *= 2
    h = a * h_ref[0:1, :] + u
    o_ref[...] = (h * gg_ref[...]).astype(o_ref.dtype)
    last = h[tm - 1:tm, :]
    h_ref[...] = jnp.broadcast_to(last, h_ref.shape)
    hl_ref[...] = last


def _rglru(x, gg, hist0, h0, w, b, wa, ba, wx, bx, lam):
    bsz, t, c = x.shape
    tm = min(SEQ_TILE, t)
    nt = t // tm
    has_prev = nt > 1
    halo = CONVC_HALO
    r = tm // halo if has_prev else 1
    tile = pl.BlockSpec((None, tm, c), lambda bi, i: (bi, i, 0))
    in_specs = [tile]
    args = [x]
    if has_prev:
        in_specs.append(pl.BlockSpec((None, halo, c), lambda bi, i: (bi, jnp.maximum(i * r - 1, 0), 0)))
        args.append(x)
    in_specs += [pl.BlockSpec((None, halo, c), lambda bi, i: (bi, 0, 0)), tile,
                 pl.BlockSpec((None, 1, c), lambda bi, i: (bi, 0, 0))]
    args += [hist0, gg, h0]
    consts = [w, b, wa, ba, wx, bx, lam]
    in_specs += [_const_spec(a.shape) for a in consts]
    args += consts
    return pl.pallas_call(
        functools.partial(_rglru_body, tm=tm, has_prev=has_prev), grid=(bsz, nt),
        in_specs=in_specs,
        out_specs=(tile, pl.BlockSpec((None, 1, c), lambda bi, i: (bi, 0, 0))),
        out_shape=(jax.ShapeDtypeStruct((bsz, t, c), BF16), jax.ShapeDtypeStruct((bsz, 1, c), F32)),
        scratch_shapes=[pltpu.VMEM((halo + tm, c), F32), pltpu.VMEM((SUBLANES, c), F32)],
        compiler_params=_cparams(("parallel", "arbitrary"), 24), name="rglru")(*args)


def _merge_body(x_ref, oa_ref, ub_ref, hc_ref, gates_ref, wa_ref, wb_ref, bb_ref, wc_ref, wo_ref, g_ref, b_ref,
                wr_ref, wrl_ref, br_ref, x1_ref, x1p_ref, idx_ref, tw_ref, *, alpha):
    d = x_ref.shape[1]
    ya = _dot(oa_ref[...], wa_ref[...])
    yb = _dot(ub_ref[...], wb_ref[...]) + bb_ref[...]
    yc = _dot(hc_ref[...], wc_ref[...])
    merged = gates_ref[:, 0:d] * ya + gates_ref[:, d:2 * d] * yb + gates_ref[:, 2 * d:3 * d] * yc
    x1 = _layer_norm(alpha * x_ref[...] + _dot(merged.astype(BF16), wo_ref[...]), g_ref[...], b_ref[...])
    x1_ref[...] = x1
    words = _pack_halves(x1)
    for p in range(x1p_ref.shape[0]):
        x1p_ref[p] = words[:, p * PLANE_WIDTH:(p + 1) * PLANE_WIDTH]
    x_hi = x1.astype(BF16)
    x_lo = (x1 - x_hi.astype(F32)).astype(BF16)
    logits = (_dot(x_hi, wr_ref[...]) + (_dot(x_lo, wr_ref[...]) + _dot(x_hi, wrl_ref[...]))) + br_ref[...]
    lane = lax.broadcasted_iota(jnp.int32, logits.shape, 1)
    vals = logits
    idx_out = jnp.zeros(logits.shape, jnp.int32)
    val_out = jnp.full(logits.shape, NEG_BIG, F32)
    for k in range(TOP_K):
        m = jnp.max(vals, axis=-1, keepdims=True)
        idx = jnp.min(jnp.where(vals == m, lane, LANES), axis=-1, keepdims=True)
        idx_out = jnp.where(lane == k, idx, idx_out)
        val_out = jnp.where(lane == k, m, val_out)
        vals = jnp.where(lane == idx, -jnp.inf, vals)
    e = jnp.exp(val_out - jnp.max(val_out, axis=-1, keepdims=True))
    e = jnp.where(lane < TOP_K, e, 0.0)
    idx_ref[...] = idx_out[:, :TOP_K]
    tw_ref[...] = (e / jnp.sum(e, axis=-1, keepdims=True))[:, :TOP_K]


def _merge(x, oa, ub, hc, gates, wts, alpha):
    n, d = x.shape
    tm = min(MERGE_TILE, n)

    def row(c):
        return pl.BlockSpec((tm, c), lambda i: (i, 0))

    return pl.pallas_call(
        functools.partial(_merge_body, alpha=alpha), grid=(n // tm,),
        in_specs=[row(d), row(oa.shape[1]), row(ub.shape[1]), row(hc.shape[1]), row(gates.shape[1])]
        + [_const_spec(a.shape) for a in wts],
        out_specs=(row(d), pl.BlockSpec((d // 2 // PLANE_WIDTH, tm, PLANE_WIDTH), lambda i: (0, i, 0)),
                   row(TOP_K), row(TOP_K)),
        out_shape=(jax.ShapeDtypeStruct((n, d), F32),
                   jax.ShapeDtypeStruct((d // 2 // PLANE_WIDTH, n, PLANE_WIDTH), jnp.int32),
                   jax.ShapeDtypeStruct((n, TOP_K), jnp.int32), jax.ShapeDtypeStruct((n, TOP_K), F32)),
        compiler_params=_cparams(("parallel",), 40), name="merge_router")(x, oa, ub, hc, gates, *wts)


def _moe_body(te_ref, tf_ref, tr_ref, xs_ref, wgu_ref, bgu_ref, wd_ref, bd_ref, ys_ref, wgub_ref, wdb_ref):
    del te_ref
    i = pl.program_id(0)
    n_planes, tm, pw = xs_ref.shape

    @pl.when(tf_ref[i] == 1)
    def _():
        wgub_ref[...] = wgu_ref[...].astype(BF16)
        wdb_ref[...] = wd_ref[...].astype(BF16)

    @pl.when(tr_ref[i] > 0)
    def _():
        dff = wd_ref.shape[0]
        live = lax.broadcasted_iota(jnp.int32, (tm, pw), 0) < tr_ref[i]
        half = n_planes * pw
        gu = bgu_ref[...]
        for p in range(n_planes):
            lo, hi = _unpack_halves(jnp.where(live, xs_ref[p], 0))
            gu = gu + _dot(lo.astype(BF16), wgub_ref[p * pw:(p + 1) * pw, :])
            gu = gu + _dot(hi.astype(BF16), wgub_ref[half + p * pw:half + (p + 1) * pw, :])
        gate = jnp.minimum(gu[:, :dff], SWIGLU_LIMIT)
        up = jnp.clip(gu[:, dff:], -SWIGLU_LIMIT, SWIGLU_LIMIT)
        hid = (up + 1.0) * gate * jax.nn.sigmoid(SWIGLU_ALPHA * gate)
        words = _pack_halves(_dot(hid.astype(BF16), wdb_ref[...]) + bd_ref[...])
        for p in range(n_planes):
            ys_ref[p] = words[:, p * pw:(p + 1) * pw]

    @pl.when(tr_ref[i] == 0)
    def _():
        ys_ref[...] = jnp.zeros_like(ys_ref)


def _moe_experts(layer, tile_expert, tile_first, tile_rows, xs, w_gate_up, b_gate_up, w_down, b_down):
    n_planes, r, pw = xs.shape
    d = 2 * n_planes * pw
    tm = MOE_TILE
    n_exp, _, dgu = w_gate_up.shape[1:]
    dff = w_down.shape[2]
    bgu = b_gate_up.reshape(-1, 1, dgu)
    bd = b_down.reshape(-1, 1, d)
    grid_spec = pltpu.PrefetchScalarGridSpec(
        num_scalar_prefetch=3, grid=(r // tm,),
        in_specs=[pl.BlockSpec((n_planes, tm, pw), lambda i, te, tf, tr: (0, i, 0)),
                  pl.BlockSpec((None, None, d, dgu), lambda i, te, tf, tr: (layer, te[i], 0, 0)),
                  pl.BlockSpec((None, 1, dgu), lambda i, te, tf, tr: (layer * n_exp + te[i], 0, 0)),
                  pl.BlockSpec((None, None, dff, d), lambda i, te, tf, tr: (layer, te[i], 0, 0)),
                  pl.BlockSpec((None, 1, d), lambda i, te, tf, tr: (layer * n_exp + te[i], 0, 0))],
        out_specs=pl.BlockSpec((n_planes, tm, pw), lambda i, te, tf, tr: (0, i, 0)),
        scratch_shapes=[pltpu.VMEM((d, dgu), BF16), pltpu.VMEM((dff, d), BF16)])
    return pl.pallas_call(
        _moe_body, grid_spec=grid_spec, out_shape=jax.ShapeDtypeStruct((n_planes, r, pw), jnp.int32),
        compiler_params=_cparams(("arbitrary",), 56), name="moe_experts")(
            tile_expert, tile_first, tile_rows, xs, w_gate_up, bgu, w_down, bd)


def _combine_body(x1_ref, yg_ref, tw_ref, g_ref, b_ref, o_ref, *, alpha):
    tw = tw_ref[...]
    n_planes = yg_ref.shape[1]
    lows, highs = [], []
    for p in range(n_planes):
        lo_acc = hi_acc = None
        for k in range(TOP_K):
            lo, hi = _unpack_halves(yg_ref[k, p])
            wk = tw[:, k:k + 1]
            lo_acc = wk * lo if lo_acc is None else lo_acc + wk * lo
            hi_acc = wk * hi if hi_acc is None else hi_acc + wk * hi
        lows.append(lo_acc)
        highs.append(hi_acc)
    y = jnp.concatenate(lows + highs, axis=1)
    o_ref[...] = _layer_norm(alpha * x1_ref[...] + y, g_ref[...], b_ref[...])


def _combine(x1, yg, tw, g, b, alpha):
    n, d = x1.shape
    n_planes, pw = yg.shape[1], yg.shape[3]
    tm = TOKEN_TILE
    return pl.pallas_call(
        functools.partial(_combine_body, alpha=alpha), grid=(n // tm,),
        in_specs=[pl.BlockSpec((tm, d), lambda i: (i, 0)),
                  pl.BlockSpec((TOP_K, n_planes, tm, pw), lambda i: (0, 0, i, 0)),
                  pl.BlockSpec((tm, TOP_K), lambda i: (i, 0)), _const_spec(g.shape), _const_spec(b.shape)],
        out_specs=pl.BlockSpec((tm, d), lambda i: (i, 0)), out_shape=jax.ShapeDtypeStruct((n, d), F32),
        compiler_params=_cparams(("parallel",), 32), name="moe_combine_norm")(x1, yg, tw, g, b)


def _dispatch_body(idx_ref, pos_ref, meta_ref, cnt_ref, base_ref):
    ph = pl.program_id(0)
    i = pl.program_id(1)
    tb = idx_ref.shape[0]
    lane = lax.broadcasted_iota(jnp.int32, (tb, LANES), 1)
    idx = idx_ref[...]
    onehot = [(idx[:, k:k + 1] == lane).astype(F32) for k in range(TOP_K)]
    colsum = [jnp.sum(o, axis=0, keepdims=True) for o in onehot]

    @pl.when(jnp.logical_and(ph == 0, i == 0))
    def _():
        cnt_ref[...] = jnp.zeros_like(cnt_ref)

    @pl.when(ph == 0)
    def _():
        cnt_ref[...] = cnt_ref[...] + (colsum[0] + colsum[1] + colsum[2] + colsum[3])

    @pl.when(jnp.logical_and(ph == 1, i == 0))
    def _():
        cnt = cnt_ref[...]
        ptiles = jnp.floor((cnt + (MOE_TILE - 1)) * (1.0 / MOE_TILE))
        tile_end = _lane_cumsum(ptiles)
        base_ref[...] = (tile_end - ptiles) * MOE_TILE
        row = lax.broadcasted_iota(jnp.int32, meta_ref.shape, 0)
        meta_ref[...] = jnp.where(row == 0, tile_end[0:1, :], jnp.where(row == 1, cnt[0:1, :], 0.0)).astype(jnp.int32)

    @pl.when(ph == 1)
    def _():
        earlier = (lax.broadcasted_iota(jnp.int32, (tb, tb), 0) > lax.broadcasted_iota(jnp.int32, (tb, tb), 1))
        earlier = earlier.astype(BF16)
        base = base_ref[0:1, :]
        slot = lax.broadcasted_iota(jnp.int32, (tb, TOP_K), 1)
        pos = jnp.zeros((tb, TOP_K), F32)
        for k in range(TOP_K):
            seen = _dot(earlier, onehot[k].astype(BF16)) + base
            pos = jnp.where(slot == k, jnp.sum(onehot[k] * seen, axis=1, keepdims=True), pos)
            base = base + colsum[k]
        base_ref[...] = jnp.broadcast_to(base, base_ref.shape)
        pos_ref[...] = pos.astype(jnp.int32)


def _dispatch(top_idx):
    n = top_idx.shape[0]
    tm = MOE_TILE
    n_tiles = -(-(n * TOP_K) // tm) + N_EXPERTS
    tb = next(t for t in DISPATCH_TILES if n % t == 0)
    pos, meta = pl.pallas_call(
        _dispatch_body, grid=(2, n // tb),
        in_specs=[pl.BlockSpec((tb, TOP_K), lambda ph, i: (i, 0))],
        out_specs=(pl.BlockSpec((tb, TOP_K), lambda ph, i: (i * ph, 0)), _const_spec((SUBLANES, LANES))),
        out_shape=(jax.ShapeDtypeStruct((n, TOP_K), jnp.int32), jax.ShapeDtypeStruct((SUBLANES, LANES), jnp.int32)),
        scratch_shapes=[pltpu.VMEM((SUBLANES, LANES), F32), pltpu.VMEM((SUBLANES, LANES), F32)],
        compiler_params=_cparams(("arbitrary", "arbitrary"), 32), name="moe_dispatch")(top_idx)
    tile_end, cnt = meta[0, :N_EXPERTS], meta[1, :N_EXPERTS]
    tiles = jnp.arange(n_tiles, dtype=jnp.int32)
    te = jnp.minimum(jnp.sum((tiles[:, None] >= tile_end[None, :]).astype(jnp.int32), axis=1), N_EXPERTS - 1)
    tile_start = tile_end - (cnt + tm - 1) // tm
    tr = jnp.clip(cnt[te] - (tiles - tile_start[te]) * tm, 0, tm) * (tiles < tile_end[-1]).astype(jnp.int32)
    tf = jnp.concatenate([jnp.ones((1,), jnp.int32), (te[1:] != te[:-1]).astype(jnp.int32)])
    return pos, te, tf, tr, n_tiles * tm


def _sc_mesh():
    return plsc.VectorSubcoreMesh(core_axis_name="c", subcore_axis_name="s")


def _sc_gather_rows(table, idx):
    m = idx.shape[0]
    w = table.shape[1]

    @functools.partial(pl.kernel, out_type=jax.ShapeDtypeStruct((m, w), table.dtype), mesh=_sc_mesh())
    def gather(t_hbm, i_hbm, o_hbm):
        def body(i_vmem, o_vmem):
            pltpu.sync_copy(t_hbm.at[i_vmem.at[0]], o_vmem)

        pltpu.emit_pipeline(
            body, grid=(m // SC_WINDOW,),
            in_specs=[pl.BlockSpec((1, SC_WINDOW), index_map=lambda i: (0, i))],
            out_specs=[pl.BlockSpec((SC_WINDOW, w), index_map=lambda i: (i, 0))],
            core_axis_name=("c", "s"), dimension_semantics=(pltpu.PARALLEL,))(i_hbm, o_hbm)

    return gather(table, idx.reshape(1, m))


def _sc_scatter_rows(rows, idx, n_out):
    n_src, w = rows.shape
    m = idx.shape[0]
    src_blocks = n_src // SC_WINDOW

    @functools.partial(pl.kernel, out_type=jax.ShapeDtypeStruct((n_out, w), rows.dtype), mesh=_sc_mesh(),
                       scratch_types=[])
    def scatter(x_hbm, i_hbm, o_hbm):
        def body(x_vmem, i_vmem):
            pltpu.sync_copy(x_vmem, o_hbm.at[i_vmem.at[0]])

        pltpu.emit_pipeline(
            body, grid=(m // SC_WINDOW,),
            in_specs=[pl.BlockSpec((SC_WINDOW, w), index_map=lambda i: (lax.rem(i, src_blocks), 0)),
                      pl.BlockSpec((1, SC_WINDOW), index_map=lambda i: (0, i))],
            out_specs=[], core_axis_name=("c", "s"), dimension_semantics=(pltpu.PARALLEL,))(x_hbm, i_hbm)

    return scatter(rows, idx.reshape(1, m))


def _plane_rows(pos, n_planes, rows_per_plane):
    off = jnp.arange(n_planes, dtype=jnp.int32) * rows_per_plane
    return (pos.T[:, None, :] + off[None, :, None]).reshape(-1)


def _pad_rows(a, rows):
    return jnp.pad(a, ((0, 0), (rows - a.shape[1], 0), (0, 0)))


def _block_diag(w):
    h, bd, _ = w.shape
    eye = jnp.eye(h, dtype=w.dtype)
    return jnp.einsum("hij,hg->higj", w, eye).reshape(h * bd, h * bd)


def kernel(x_prompt, x_sample, cache_k, cache_v, cache_logf, state_conv_b, state_conv_c, state_h, page_table,
           w_in, b_f, w_att_o, conv_b_w, conv_b_b, lnb_g, lnb_b, w_conv_o, b_conv_o, conv_c_w, conv_c_b,
           w_rg_a, b_rg_a, w_rg_x, b_rg_x, rg_lambda, w_rg_o, w_out, ln1_g, ln1_b, w_router, b_router,
           w_gate_up, b_gate_up, w_down, b_down, ln2_g, ln2_b):
    bp, seq, d = x_prompt.shape
    bs, ns, _ = x_sample.shape
    depth, n_pool, page = cache_k.shape[:3]
    w_conv = state_conv_b.shape[-1]
    w_rg = state_conv_c.shape[-1]
    alpha = (2 * depth) ** 0.25
    n_p, n_s = bp * seq, bs * ns
    off_f = 3 * W_ATT
    off_glu = off_f + H_ATT
    off_rg = off_glu + 2 * w_conv
    off_gate = off_rg + 2 * w_rg

    ckt = cache_k.transpose(0, 1, 3, 4, 2).reshape(depth, n_pool, W_ATT, page)
    cvt = cache_v.transpose(0, 1, 3, 4, 2).reshape(depth, n_pool, W_ATT, page)
    cft = cache_logf.transpose(0, 1, 3, 2)
    head_eye = jnp.eye(H_ATT, dtype=BF16)

    xp = x_prompt.reshape(n_p, d)
    xs = x_sample.reshape(n_s, d)
    outs = {k: [] for k in ("ks", "vs", "fs", "cbp", "cbs", "ccp", "ccs", "hp", "hs")}
    stacked = None

    for l in range(depth):
        wl = w_in[l]
        wf = wl[:, off_f:off_glu]
        in_w = (wl[:, :2 * W_ATT].astype(BF16), wl[:, :off_f].T.astype(BF16),
                jnp.pad(wf, ((0, 0), (0, LANES - H_ATT))).astype(BF16), wf.T.astype(BF16),
                b_f[l].reshape(1, H_ATT), b_f[l].reshape(H_ATT, 1),
                wl[:, off_glu:off_rg].astype(BF16), wl[:, off_rg:off_gate].astype(BF16),
                wl[:, off_gate:].astype(BF16))
        cbw = jnp.broadcast_to(conv_b_w[l][:, None, :], (CONV_WIDTH, SUBLANES, w_conv))
        ccw = jnp.pad(conv_c_w[l], ((0, CONVC_HALO - RG_CONV_WIDTH), (0, 0)))
        rg_w = (ccw, conv_c_b[l].reshape(1, -1), _block_diag(w_rg_a[l]).astype(BF16), b_rg_a[l].reshape(1, -1),
                _block_diag(w_rg_x[l]).astype(BF16), b_rg_x[l].reshape(1, -1), rg_lambda[l].reshape(1, -1))
        cb_w = (cbw, conv_b_b[l].reshape(1, -1), lnb_g[l].reshape(1, -1), lnb_b[l].reshape(1, -1))
        wr = jnp.pad(w_router[l], ((0, 0), (0, LANES - N_EXPERTS)))
        br = jnp.pad(b_router[l], (0, LANES - N_EXPERTS), constant_values=NEG_BIG).reshape(1, LANES)
        mg_w = (w_att_o[l].astype(BF16), w_conv_o[l].astype(BF16), b_conv_o[l].reshape(1, -1),
                w_rg_o[l].astype(BF16), w_out[l].astype(BF16), ln1_g[l].reshape(1, -1), ln1_b[l].reshape(1, -1),
                wr.astype(BF16), (wr - wr.astype(BF16).astype(F32)).astype(BF16), br)

        (q, kb, qtm, kt, vt, vtb, lf, lft, u, rgx, gg, gates) = _inproj(xp, in_w, seq, l, depth, stacked)
        stacked = (kt, vt, lft)
        oa = _fox_prompt(qtm, kb, vtb, _cumsum(lf, seq), bp, seq)
        u3 = u.reshape(bp, seq, w_conv)
        ub = _convb(u3, jnp.zeros((bp, CONVB_HALO, w_conv), F32), *cb_w)
        rgx3 = rgx.reshape(bp, seq, w_rg)
        hc, hl = _rglru(rgx3, gg.reshape(bp, seq, w_rg), jnp.zeros((bp, CONVC_HALO, w_rg), F32),
                        jnp.zeros((bp, 1, w_rg), F32), *rg_w)
        x1_p, x1p, tip, twp = _merge(xp, oa, ub.reshape(n_p, w_conv), hc.reshape(n_p, w_rg), gates, mg_w, alpha)
        outs["cbp"].append(u3[:, seq - (CONV_WIDTH - 1):])
        outs["ccp"].append(rgx3[:, seq - (RG_CONV_WIDTH - 1):])
        outs["hp"].append(hl.reshape(bp, w_rg))

        (q, kb, qtm, kt, vt, vtb, lf, lft, u, rgx, gg, gates) = _inproj(xs, in_w, n_s)
        qbd = jnp.einsum("bthd,hg->bhtgd", q.reshape(bs, ns, H_ATT, HD_ATT), head_eye)
        qbd = qbd.reshape(bs, H_ATT * ns, W_ATT)

        def new_page(a):
            a = a.reshape(a.shape[0], bs, ns).transpose(1, 0, 2)
            return jnp.pad(a, ((0, 0), (0, 0), (0, page - ns)))

        kt, vt, lft = kt[0, 0], vt[0, 0], lft[0, 0]
        oa = _fox_sample(l, page_table, qbd, ckt, cvt, cft, new_page(kt), new_page(vt), new_page(lft), ns)
        u3 = u.reshape(bs, ns, w_conv)
        ub = _convb(u3, _pad_rows(state_conv_b[l], CONVB_HALO), *cb_w)
        rgx3 = rgx.reshape(bs, ns, w_rg)
        hc, hl = _rglru(rgx3, gg.reshape(bs, ns, w_rg), _pad_rows(state_conv_c[l], CONVC_HALO),
                        state_h[l].reshape(bs, 1, w_rg), *rg_w)
        x1_s, x1s, tis, tws = _merge(xs, oa.reshape(n_s, W_ATT), ub.reshape(n_s, w_conv), hc.reshape(n_s, w_rg),
                                     gates, mg_w, alpha)
        outs["ks"].append(kt.T.reshape(bs, ns, H_ATT, HD_ATT))
        outs["vs"].append(vt.T.reshape(bs, ns, H_ATT, HD_ATT))
        outs["fs"].append(lf.reshape(bs, ns, H_ATT))
        outs["cbs"].append(jnp.concatenate([state_conv_b[l], u3], axis=1)[:, ns:])
        outs["ccs"].append(jnp.concatenate([state_conv_c[l], rgx3], axis=1)[:, ns:])
        outs["hs"].append(hl.reshape(bs, w_rg))

        pos, te, tf, tr, n_rows = _dispatch(jnp.concatenate([tip, tis], axis=0))
        x1_all = jnp.concatenate([x1p, x1s], axis=1)
        n_planes, n_all, pw = x1_all.shape
        xs_rows = _sc_scatter_rows(x1_all.reshape(n_planes * n_all, pw), _plane_rows(pos, n_planes, n_rows),
                                   n_planes * n_rows)
        ys = _moe_experts(l, te, tf, tr, xs_rows.reshape(n_planes, n_rows, pw), w_gate_up, b_gate_up, w_down, b_down)
        ys = ys.reshape(n_planes * n_rows, pw)
        g2, b2 = ln2_g[l].reshape(1, -1), ln2_b[l].reshape(1, -1)
        yg = _sc_gather_rows(ys, _plane_rows(pos[:n_p], n_planes, n_rows)).reshape(TOP_K, n_planes, n_p, pw)
        xp = _combine(x1_p, yg, twp, g2, b2, alpha)
        yg = _sc_gather_rows(ys, _plane_rows(pos[n_p:], n_planes, n_rows)).reshape(TOP_K, n_planes, n_s, pw)
        xs = _combine(x1_s, yg, tws, g2, b2, alpha)

    st = {k: jnp.stack(v, 0) for k, v in outs.items()}
    kt, vt, lft = stacked
    st["kp"] = kt.reshape(depth, bp, H_ATT, HD_ATT, seq).transpose(0, 1, 4, 2, 3)
    st["vp"] = vt.reshape(depth, bp, H_ATT, HD_ATT, seq).transpose(0, 1, 4, 2, 3)
    st["fp"] = lft.transpose(0, 1, 3, 2)
    return (xp.reshape(bp, seq, d), xs.reshape(bs, ns, d), st["kp"], st["vp"], st["fp"], st["ks"], st["vs"],
            st["fs"], st["cbp"], st["cbs"], st["ccp"], st["ccs"], st["hp"], st["hs"])
```

```python
import functools
import math

import jax
import jax.numpy as jnp
from jax import lax
from jax.experimental import pallas as pl
from jax.experimental.pallas import tpu as pltpu
from jax.experimental.pallas import tpu_sc as plsc

F32 = jnp.float32
BF16 = jnp.bfloat16

H_ATT = 8
HD_ATT = 64
W_ATT = H_ATT * HD_ATT
CONV_WIDTH = 31
RG_CONV_WIDTH = 4
H_RG = 8
RG_C = 8.0
N_EXPERTS = 32
TOP_K = 4
SWIGLU_LIMIT = 7.0
SWIGLU_ALPHA = 1.702
LN_EPS = 1e-5
N_BRANCH = 3

LANES = 128
SUBLANES = 8
NEG_BIG = -1e30

TOKEN_TILE = 256
MERGE_TILE = 512
ATT_TILE = 512
CUMSUM_TILE = 512
SEQ_TILE = 256
CONVB_HALO = 32
CONVC_HALO = 8
CONV_CHUNK = 32
PAGES_PER_STEP = 16
MOE_TILE = 256
PLANE_WIDTH = 256
SC_WINDOW = 128
DISPATCH_TILES = (1280, 1024, 768, 512, 256)


def _cparams(sem, vmem_mb):
    return pltpu.CompilerParams(dimension_semantics=sem, vmem_limit_bytes=vmem_mb << 20)


def _const_spec(shape):
    nd = len(shape)
    return pl.BlockSpec(shape, lambda *_: (0,) * nd)


def _log_sigmoid(z):
    return jnp.minimum(z, 0.0) - jnp.log1p(jnp.exp(-jnp.abs(z)))


def _softplus(z):
    return jnp.maximum(z, 0.0) + jnp.log1p(jnp.exp(-jnp.abs(z)))


def _layer_norm(x, g, b):
    mu = jnp.mean(x, axis=-1, keepdims=True)
    xc = x - mu
    var = jnp.mean(xc * xc, axis=-1, keepdims=True)
    return xc * lax.rsqrt(var + LN_EPS) * g + b


def _pack_halves(x):
    n = x.shape[1] // 2
    lo = lax.bitcast_convert_type(x[:, :n].astype(BF16).astype(F32), jnp.uint32)
    hi = lax.bitcast_convert_type(x[:, n:].astype(BF16).astype(F32), jnp.uint32)
    return lax.bitcast_convert_type(hi | (lo >> 16), jnp.int32)


def _unpack_halves(w):
    u = lax.bitcast_convert_type(w, jnp.uint32)
    return (lax.bitcast_convert_type(u << 16, F32),
            lax.bitcast_convert_type(u & jnp.uint32(0xFFFF0000), F32))


def _dot(a, b):
    return jnp.dot(a, b, preferred_element_type=F32)


def _dot_nt(a, b, precision=None):
    return lax.dot_general(a, b, (((1,), (1,)), ((), ())), preferred_element_type=F32, precision=precision)


def _store_slab(ref, layer, val):
    if len(ref.shape) == val.ndim:
        ref[...] = val
    else:
        for l in range(ref.shape[0]):
            ref[l] = val if l == layer else jnp.zeros_like(val)


def _inproj_body(x_ref, wqk_ref, wt_ref, wf_ref, wft_ref, bf_ref, bft_ref, wglu_ref, wrg_ref, wgate_ref,
                 q_ref, kb_ref, qtm_ref, kt_ref, vt_ref, vtb_ref, lf_ref, lft_ref, u_ref, rgx_ref, gg_ref, gates_ref,
                 *, layer):
    xb = x_ref[...].astype(BF16)
    w = W_ATT
    scale = HD_ATT ** -0.5
    qk = _dot(xb, wqk_ref[...])
    q_ref[...] = (qk[:, :w] * scale).astype(BF16)
    kb_ref[...] = qk[:, w:].astype(BF16)
    t = _dot_nt(wt_ref[...], xb)
    _store_slab(kt_ref, layer, t[w:2 * w, :])
    vt = t[2 * w:, :]
    _store_slab(vt_ref, layer, vt)
    vtb_ref[...] = vt.astype(BF16)
    qt = (t[:w, :] * scale).astype(BF16)
    pair = 2 * HD_ATT
    lower = lax.broadcasted_iota(jnp.int32, (pair, qt.shape[1]), 0) < HD_ATT
    for h in range(H_ATT):
        slab = qt[(h // 2) * pair:(h // 2 + 1) * pair, :]
        qtm_ref[h] = jnp.where(lower if h % 2 == 0 else jnp.logical_not(lower), slab, jnp.zeros_like(slab))
    zf = _dot(xb, wf_ref[...])
    lf_ref[...] = _log_sigmoid(zf[:, :H_ATT] + bf_ref[...])
    _store_slab(lft_ref, layer, _log_sigmoid(_dot_nt(wft_ref[...], xb) + bft_ref[...]))
    glu = _dot(xb, wglu_ref[...])
    c = glu.shape[1] // 2
    u_ref[...] = glu[:, :c] * jax.nn.sigmoid(glu[:, c:])
    rg = _dot(xb, wrg_ref[...])
    c = rg.shape[1] // 2
    rgx_ref[...] = rg[:, :c]
    gg_ref[...] = jax.nn.gelu(rg[:, c:]).astype(gg_ref.dtype)
    d = gates_ref.shape[1] // N_BRANCH
    for j in range(N_BRANCH):
        gate = jax.nn.sigmoid(_dot(xb, wgate_ref[:, j * d:(j + 1) * d]))
        gates_ref[:, j * d:(j + 1) * d] = gate.astype(gates_ref.dtype)


_INPROJ_STACKED_OUTS = (3, 4, 7)


def _inproj_carry_body(*refs, n_in, layer):
    _inproj_body(*refs[:n_in], *refs[n_in + len(_INPROJ_STACKED_OUTS):], layer=layer)


def _inproj(x, wts, seq, layer=0, depth=1, carried=None):
    n, d = x.shape
    tm = TOKEN_TILE
    wglu, wrg, wgate = wts[-3:]
    wc, wr = wglu.shape[1] // 2, wrg.shape[1] // 2
    bps = seq // tm

    def row(c):
        return pl.BlockSpec((tm, c), lambda i: (i, 0))

    def col(r):
        return pl.BlockSpec((None, r, tm), lambda i: (i // bps, 0, i % bps))

    def slab(r):
        if carried is None:
            return pl.BlockSpec((depth, None, r, tm), lambda i: (0, i // bps, 0, i % bps))
        return pl.BlockSpec((None, None, r, tm), lambda i: (layer, i // bps, 0, i % bps))

    def slabshape(r):
        return jax.ShapeDtypeStruct((depth, n // seq, r, seq), F32)

    out_shape = (
        jax.ShapeDtypeStruct((n, W_ATT), BF16), jax.ShapeDtypeStruct((n, W_ATT), BF16),
        jax.ShapeDtypeStruct((H_ATT, 2 * HD_ATT, n), BF16),
        slabshape(W_ATT), slabshape(W_ATT), jax.ShapeDtypeStruct((n // seq, W_ATT, seq), BF16),
        jax.ShapeDtypeStruct((n, H_ATT), F32), slabshape(H_ATT), jax.ShapeDtypeStruct((n, wc), F32),
        jax.ShapeDtypeStruct((n, wr), F32), jax.ShapeDtypeStruct((n, wr), BF16),
        jax.ShapeDtypeStruct((n, wgate.shape[1]), BF16))
    out_specs = (row(W_ATT), row(W_ATT), pl.BlockSpec((H_ATT, 2 * HD_ATT, tm), lambda i: (0, 0, i)),
                 slab(W_ATT), slab(W_ATT), col(W_ATT),
                 row(H_ATT), slab(H_ATT), row(wc), row(wr), row(wr), row(wgate.shape[1]))
    in_specs = [row(d)] + [_const_spec(a.shape) for a in wts]
    args = [x, *wts]
    if carried is None:
        body, aliases = functools.partial(_inproj_body, layer=layer), {}
    else:
        body = functools.partial(_inproj_carry_body, n_in=len(args), layer=layer)
        aliases = {len(args) + j: o for j, o in enumerate(_INPROJ_STACKED_OUTS)}
        in_specs += [pl.BlockSpec(memory_space=pl.ANY)] * len(carried)
        args += list(carried)
    return pl.pallas_call(
        body, grid=(n // tm,), in_specs=in_specs, out_specs=out_specs, out_shape=out_shape,
        input_output_aliases=aliases,
        compiler_params=_cparams(("parallel",), 56), name="inproj")(*args)


def _lane_cumsum(c):
    n = c.shape[1]
    lane = lax.broadcasted_iota(jnp.int32, c.shape, 1)
    d = 1
    while d < n:
        c = c + jnp.where(lane >= d, pltpu.roll(c, d, axis=1), 0.0)
        d *= 2
    return c


def _cumsum_body(f_ref, c_ref, carry_ref, *, blocks_per_seq):
    @pl.when(pl.program_id(0) % blocks_per_seq == 0)
    def _():
        carry_ref[...] = jnp.zeros_like(carry_ref)

    c = f_ref[...]
    n = c.shape[0]
    row = lax.broadcasted_iota(jnp.int32, c.shape, 0)
    d = 1
    while d < n:
        c = c + jnp.where(row >= d, pltpu.roll(c, d, axis=0), 0.0)
        d *= 2
    c = c + carry_ref[0:1, :]
    carry_ref[...] = jnp.broadcast_to(c[n - 1:n, :], carry_ref.shape)
    for h in range(H_ATT):
        c_ref[h] = jnp.broadcast_to(c[:, h:h + 1], (n, LANES))


def _cumsum(lf, seq):
    n, h = lf.shape
    tc = min(CUMSUM_TILE, seq)
    return pl.pallas_call(
        functools.partial(_cumsum_body, blocks_per_seq=seq // tc), grid=(n // tc,),
        in_specs=[pl.BlockSpec((tc, h), lambda i: (i, 0))],
        out_specs=pl.BlockSpec((h, tc, LANES), lambda i: (0, i, 0)),
        out_shape=jax.ShapeDtypeStruct((h, n, LANES), F32),
        scratch_shapes=[pltpu.VMEM((SUBLANES, h), F32)],
        compiler_params=_cparams(("arbitrary",), 16), name="logf_cumsum")(lf)


def _fox_prompt_body(qi_ref, ki_ref, qtm_ref, k_ref, vt_ref, ck_ref, cq_ref, o_ref, m_ref, l_ref, acc_ref):
    qi = qi_ref[pl.program_id(1)]
    ki = ki_ref[pl.program_id(1)]
    tk, tq = k_ref.shape[0], qtm_ref.shape[2]
    pair = 2 * HD_ATT

    @pl.when(ki == 0)
    def _():
        m_ref[...] = jnp.full_like(m_ref, NEG_BIG)
        l_ref[...] = jnp.zeros_like(l_ref)
        acc_ref[...] = jnp.zeros_like(acc_ref)

    def update(diagonal):
        if diagonal:
            keep = lax.broadcasted_iota(jnp.int32, (tk, tq), 0) <= lax.broadcasted_iota(jnp.int32, (tk, tq), 1)
        for h in range(H_ATT):
            s = _dot(k_ref[:, (h // 2) * pair:(h // 2 + 1) * pair], qtm_ref[h])
            bias = ck_ref[h] - cq_ref[h, 0:1, :]
            s = s - jnp.concatenate([bias] * (tq // LANES), axis=1)
            if diagonal:
                s = jnp.where(keep, s, NEG_BIG)
            m_old = m_ref[h]
            m_new = jnp.maximum(m_old, jnp.max(s, axis=0, keepdims=True))
            p = jnp.exp(s - m_new)
            alpha = jnp.exp(m_old - m_new)
            l_ref[h] = alpha * l_ref[h] + jnp.sum(p, axis=0, keepdims=True)
            acc_ref[h] = alpha * acc_ref[h] + _dot(vt_ref[h * HD_ATT:(h + 1) * HD_ATT, :], p.astype(BF16))
            m_ref[h] = m_new

    @pl.when(ki < qi)
    def _():
        update(False)

    @pl.when(ki == qi)
    def _():
        update(True)
        for j in range(H_ATT // 2):
            o2 = jnp.concatenate([acc_ref[2 * j] / l_ref[2 * j], acc_ref[2 * j + 1] / l_ref[2 * j + 1]], axis=0)
            o_ref[:, j * pair:(j + 1) * pair] = o2.T.astype(o_ref.dtype)


def _fox_prompt(qtm, kb, vt, cb, batch, seq):
    n = kb.shape[0]
    t = min(ATT_TILE, seq)
    nb = seq // t
    pairs = [(qi, ki) for qi in range(nb) for ki in range(qi + 1)]
    qi_of = jnp.array([p[0] for p in pairs], jnp.int32)
    ki_of = jnp.array([p[1] for p in pairs], jnp.int32)
    grid_spec = pltpu.PrefetchScalarGridSpec(
        num_scalar_prefetch=2, grid=(batch, len(pairs)),
        in_specs=[pl.BlockSpec((H_ATT, 2 * HD_ATT, t), lambda b, p, qs, ks: (0, 0, b * nb + qs[p])),
                  pl.BlockSpec((t, W_ATT), lambda b, p, qs, ks: (b * nb + ks[p], 0)),
                  pl.BlockSpec((None, W_ATT, t), lambda b, p, qs, ks: (b, 0, ks[p])),
                  pl.BlockSpec((H_ATT, t, LANES), lambda b, p, qs, ks: (0, b * nb + ks[p], 0)),
                  pl.BlockSpec((H_ATT, SUBLANES, LANES),
                               lambda b, p, qs, ks: (0, (b * nb + qs[p]) * (t // SUBLANES), 0))],
        out_specs=pl.BlockSpec((t, W_ATT), lambda b, p, qs, ks: (b * nb + qs[p], 0)),
        scratch_shapes=[pltpu.VMEM((H_ATT, 1, t), F32), pltpu.VMEM((H_ATT, 1, t), F32),
                        pltpu.VMEM((H_ATT, HD_ATT, t), F32)])
    return pl.pallas_call(
        _fox_prompt_body, grid_spec=grid_spec, out_shape=jax.ShapeDtypeStruct((n, W_ATT), BF16),
        compiler_params=_cparams(("parallel", "arbitrary"), 40), name="fox_prompt")(
            qi_of, ki_of, qtm, kb, vt, cb, cb)


def _fox_sample_body(pt_ref, qbd_ref, *refs, n_pages_step, n_new):
    del pt_ref
    np_ = n_pages_step
    k_refs = refs[:np_]
    v_refs = refs[np_:2 * np_]
    f_refs = refs[2 * np_:3 * np_]
    kn_ref, vn_ref, fn_ref, o_ref, m_ref, l_ref, acc_ref, carry_ref, kb_ref, vb_ref = refs[3 * np_:]
    c = pl.program_id(1)
    rows = qbd_ref.shape[0]
    page = kn_ref.shape[1]

    @pl.when(c == 0)
    def _():
        m_ref[...] = jnp.full_like(m_ref, NEG_BIG)
        l_ref[...] = jnp.zeros_like(l_ref)
        acc_ref[...] = jnp.zeros_like(acc_ref)
        carry_ref[...] = jnp.zeros_like(carry_ref)

    def decay_rows(ft):
        ct = _lane_cumsum(ft) + carry_ref[:, 0:1]
        carry_ref[...] = jnp.broadcast_to(ct[:, ct.shape[1] - 1:], carry_ref.shape)
        return jnp.concatenate([jnp.broadcast_to(ct[h:h + 1, :], (n_new, ct.shape[1])) for h in range(H_ATT)], axis=0)

    def update(kb, vb, cexp, keep):
        s = _dot(qbd_ref[...], kb) - cexp
        if keep is not None:
            s = jnp.where(keep, s, NEG_BIG)
        m_old = m_ref[...]
        m_new = jnp.maximum(m_old, jnp.max(s, axis=-1, keepdims=True))
        p = jnp.exp(s - m_new)
        alpha = jnp.exp(m_old - m_new)
        l_ref[...] = alpha * l_ref[...] + jnp.sum(p, axis=-1, keepdims=True)
        acc_ref[...] = alpha * acc_ref[...] + _dot_nt(p.astype(BF16), vb)
        m_ref[...] = m_new

    for j in range(np_):
        kb_ref[:, j * page:(j + 1) * page] = k_refs[j][...].astype(BF16)
        vb_ref[:, j * page:(j + 1) * page] = v_refs[j][...].astype(BF16)
    ft = jnp.concatenate([r[...] for r in f_refs], axis=1)
    update(kb_ref[...], vb_ref[...], decay_rows(ft), None)

    @pl.when(c == pl.num_programs(1) - 1)
    def _():
        tok = lax.broadcasted_iota(jnp.int32, (rows, page), 0) % n_new
        key = lax.broadcasted_iota(jnp.int32, (rows, page), 1)
        update(kn_ref[...].astype(BF16), vn_ref[...].astype(BF16), decay_rows(fn_ref[...]), key <= tok)
        acc = acc_ref[...] / l_ref[...]
        for h in range(H_ATT):
            o_ref[:, h * HD_ATT:(h + 1) * HD_ATT] = acc[h * n_new:(h + 1) * n_new,
                                                        h * HD_ATT:(h + 1) * HD_ATT].astype(o_ref.dtype)


def _fox_sample(layer, page_table, qbd, cache_kt, cache_vt, cache_ft, kt_new, vt_new, ft_new, n_new):
    bsz, n_pages = page_table.shape
    page = cache_kt.shape[3]
    np_ = PAGES_PER_STEP
    while n_pages % np_:
        np_ //= 2
    rows = qbd.shape[1]

    def page_spec(r, j):
        return pl.BlockSpec((None, None, r, page), lambda b, c, pt: (layer, pt[b, c * np_ + j], 0, 0))

    def seq_spec(r, w):
        return pl.BlockSpec((None, r, w), lambda b, c, pt: (b, 0, 0))

    in_specs = ([seq_spec(rows, W_ATT)]
                + [page_spec(W_ATT, j) for j in range(np_)] + [page_spec(W_ATT, j) for j in range(np_)]
                + [page_spec(H_ATT, j) for j in range(np_)]
                + [seq_spec(W_ATT, page), seq_spec(W_ATT, page), seq_spec(H_ATT, page)])
    grid_spec = pltpu.PrefetchScalarGridSpec(
        num_scalar_prefetch=1, grid=(bsz, n_pages // np_), in_specs=in_specs,
        out_specs=seq_spec(n_new, W_ATT),
        scratch_shapes=[pltpu.VMEM((rows, 1), F32), pltpu.VMEM((rows, 1), F32), pltpu.VMEM((rows, W_ATT), F32),
                        pltpu.VMEM((H_ATT, LANES), F32),
                        pltpu.VMEM((W_ATT, np_ * page), BF16), pltpu.VMEM((W_ATT, np_ * page), BF16)])
    return pl.pallas_call(
        functools.partial(_fox_sample_body, n_pages_step=np_, n_new=n_new),
        grid_spec=grid_spec, out_shape=jax.ShapeDtypeStruct((bsz, n_new, W_ATT), BF16),
        compiler_params=_cparams(("parallel", "arbitrary"), 40), name="fox_sample")(
            page_table, qbd, *([cache_kt] * np_), *([cache_vt] * np_), *([cache_ft] * np_), kt_new, vt_new, ft_new)


def _convb_body(*refs, tm, has_prev):
    if has_prev:
        u_ref, prev_ref, hist0_ref, w_ref, b_ref, g_ref, beta_ref, o_ref, ext_ref = refs
        hist = jnp.where(pl.program_id(1) == 0, hist0_ref[...], prev_ref[...])
    else:
        u_ref, hist0_ref, w_ref, b_ref, g_ref, beta_ref, o_ref, ext_ref = refs
        hist = hist0_ref[...]
    halo = CONVB_HALO
    ext_ref[0:halo, :] = hist
    ext_ref[halo:halo + tm, :] = u_ref[...]
    lead = halo - (CONV_WIDTH - 1)
    ch = min(CONV_CHUNK, tm)
    for r0 in range(0, tm, ch):
        acc = jnp.broadcast_to(b_ref[...], (ch, b_ref.shape[1]))
        for r in range(SUBLANES):
            taps = [k for k in range(CONV_WIDTH) if (k + lead) % SUBLANES == r]
            if not taps:
                continue
            win = ext_ref[r0 + r:r0 + taps[-1] + lead + ch, :]
            for k in taps:
                off = k + lead - r
                wk = jnp.concatenate([w_ref[k]] * (ch // SUBLANES), axis=0)
                acc = acc + wk * win[off:off + ch, :]
        y = _layer_norm(acc, g_ref[...], beta_ref[...])
        o_ref[r0:r0 + ch, :] = (y * jax.nn.sigmoid(y)).astype(o_ref.dtype)


def _convb(u, hist0, w, b, g, beta):
    bsz, t, c = u.shape
    tm = min(SEQ_TILE, t)
    nt = t // tm
    has_prev = nt > 1
    halo = CONVB_HALO
    r = tm // halo if has_prev else 1
    in_specs = [pl.BlockSpec((None, tm, c), lambda bi, i: (bi, i, 0))]
    args = [u]
    if has_prev:
        in_specs.append(pl.BlockSpec((None, halo, c), lambda bi, i: (bi, jnp.maximum(i * r - 1, 0), 0)))
        args.append(u)
    in_specs += [pl.BlockSpec((None, halo, c), lambda bi, i: (bi, 0, 0)),
                 _const_spec(w.shape), _const_spec(b.shape), _const_spec(g.shape), _const_spec(beta.shape)]
    args += [hist0, w, b, g, beta]
    return pl.pallas_call(
        functools.partial(_convb_body, tm=tm, has_prev=has_prev), grid=(bsz, nt),
        in_specs=in_specs, out_specs=pl.BlockSpec((None, tm, c), lambda bi, i: (bi, i, 0)),
        out_shape=jax.ShapeDtypeStruct((bsz, t, c), BF16),
        scratch_shapes=[pltpu.VMEM((halo + tm, c), F32)],
        compiler_params=_cparams(("parallel", "arbitrary"), 24), name="conformer_conv")(*args)


def _rglru_body(*refs, tm, has_prev):
    if has_prev:
        (x_ref, prev_ref, hist0_ref, gg_ref, h0_ref, w_ref, b_ref, wa_ref, ba_ref, wx_ref, bx_ref, lam_ref,
         o_ref, hl_ref, ext_ref, h_ref) = refs
        hist = jnp.where(pl.program_id(1) == 0, hist0_ref[...], prev_ref[...])
    else:
        (x_ref, hist0_ref, gg_ref, h0_ref, w_ref, b_ref, wa_ref, ba_ref, wx_ref, bx_ref, lam_ref,
         o_ref, hl_ref, ext_ref, h_ref) = refs
        hist = hist0_ref[...]

    @pl.when(pl.program_id(1) == 0)
    def _():
        h_ref[...] = jnp.broadcast_to(h0_ref[...], h_ref.shape)

    halo = CONVC_HALO
    ext_ref[0:halo, :] = hist
    ext_ref[halo:halo + tm, :] = x_ref[...]
    lead = halo - (RG_CONV_WIDTH - 1)
    xc = jnp.broadcast_to(b_ref[...], x_ref.shape)
    for k in range(RG_CONV_WIDTH):
        xc = xc + w_ref[k:k + 1, :] * ext_ref[k + lead:k + lead + tm, :]
    xcb = xc.astype(BF16)
    r = jax.nn.sigmoid(_dot(xcb, wa_ref[...]) + ba_ref[...])
    ig = jax.nn.sigmoid(_dot(xcb, wx_ref[...]) + bx_ref[...])
    log_a = -RG_C * r * _softplus(-lam_ref[...])
    a = jnp.exp(log_a)
    u = jnp.sqrt(-jnp.tanh(log_a) * (1.0 + a * a)) * (ig * xc)
    row = lax.broadcasted_iota(jnp.int32, a.shape, 0)
    d = 1
    while d < tm:
        a_sh = pltpu.roll(a, d, axis=0)
        u_sh = pltpu.roll(u, d, axis=0)
        keep = row >= d
        u = jnp.where(keep, a * u_sh + u, u)
        a = jnp.where(keep, a * a_sh, a)
        d *= 2
    h = a * h_ref[0:1, :] + u
    o_ref[...] = (h * gg_ref[...]).astype(o_ref.dtype)
    last = h[tm - 1:tm, :]
    h_ref[...] = jnp.broadcast_to(last, h_ref.shape)
    hl_ref[...] = last


def _rglru(x, gg, hist0, h0, w, b, wa, ba, wx, bx, lam):
    bsz, t, c = x.shape
    tm = min(SEQ_TILE, t)
    nt = t // tm
    has_prev = nt > 1
    halo = CONVC_HALO
    r = tm // halo if has_prev else 1
    tile = pl.BlockSpec((None, tm, c), lambda bi, i: (bi, i, 0))
    in_specs = [tile]
    args = [x]
    if has_prev:
        in_specs.append(pl.BlockSpec((None, halo, c), lambda bi, i: (bi, jnp.maximum(i * r - 1, 0), 0)))
        args.append(x)
    in_specs += [pl.BlockSpec((None, halo, c), lambda bi, i: (bi, 0, 0)), tile,
                 pl.BlockSpec((None, 1, c), lambda bi, i: (bi, 0, 0))]
    args += [hist0, gg, h0]
    consts = [w, b, wa, ba, wx, bx, lam]
    in_specs += [_const_spec(a.shape) for a in consts]
    args += consts
    return pl.pallas_call(
        functools.partial(_rglru_body, tm=tm, has_prev=has_prev), grid=(bsz, nt),
        in_specs=in_specs,
        out_specs=(tile, pl.BlockSpec((None, 1, c), lambda bi, i: (bi, 0, 0))),
        out_shape=(jax.ShapeDtypeStruct((bsz, t, c), BF16), jax.ShapeDtypeStruct((bsz, 1, c), F32)),
        scratch_shapes=[pltpu.VMEM((halo + tm, c), F32), pltpu.VMEM((SUBLANES, c), F32)],
        compiler_params=_cparams(("parallel", "arbitrary"), 24), name="rglru")(*args)


def _merge_body(x_ref, oa_ref, ub_ref, hc_ref, gates_ref, wa_ref, wb_ref, bb_ref, wc_ref, wo_ref, g_ref, b_ref,
                wr_ref, wrl_ref, br_ref, x1_ref, x1p_ref, idx_ref, tw_ref, *, alpha):
    d = x_ref.shape[1]
    ya = _dot(oa_ref[...], wa_ref[...])
    yb = _dot(ub_ref[...], wb_ref[...]) + bb_ref[...]
    yc = _dot(hc_ref[...], wc_ref[...])
    merged = gates_ref[:, 0:d] * ya + gates_ref[:, d:2 * d] * yb + gates_ref[:, 2 * d:3 * d] * yc
    x1 = _layer_norm(alpha * x_ref[...] + _dot(merged.astype(BF16), wo_ref[...]), g_ref[...], b_ref[...])
    x1_ref[...] = x1
    words = _pack_halves(x1)
    for p in range(x1p_ref.shape[0]):
        x1p_ref[p] = words[:, p * PLANE_WIDTH:(p + 1) * PLANE_WIDTH]
    x_hi = x1.astype(BF16)
    x_lo = (x1 - x_hi.astype(F32)).astype(BF16)
    logits = (_dot(x_hi, wr_ref[...]) + (_dot(x_lo, wr_ref[...]) + _dot(x_hi, wrl_ref[...]))) + br_ref[...]
    lane = lax.broadcasted_iota(jnp.int32, logits.shape, 1)
    vals = logits
    idx_out = jnp.zeros(logits.shape, jnp.int32)
    val_out = jnp.full(logits.shape, NEG_BIG, F32)
    for k in range(TOP_K):
        m = jnp.max(vals, axis=-1, keepdims=True)
        idx = jnp.min(jnp.where(vals == m, lane, LANES), axis=-1, keepdims=True)
        idx_out = jnp.where(lane == k, idx, idx_out)
        val_out = jnp.where(lane == k, m, val_out)
        vals = jnp.where(lane == idx, -jnp.inf, vals)
    e = jnp.exp(val_out - jnp.max(val_out, axis=-1, keepdims=True))
    e = jnp.where(lane < TOP_K, e, 0.0)
    idx_ref[...] = idx_out[:, :TOP_K]
    tw_ref[...] = (e / jnp.sum(e, axis=-1, keepdims=True))[:, :TOP_K]


def _merge(x, oa, ub, hc, gates, wts, alpha):
    n, d = x.shape
    tm = min(MERGE_TILE, n)

    def row(c):
        return pl.BlockSpec((tm, c), lambda i: (i, 0))

    return pl.pallas_call(
        functools.partial(_merge_body, alpha=alpha), grid=(n // tm,),
        in_specs=[row(d), row(oa.shape[1]), row(ub.shape[1]), row(hc.shape[1]), row(gates.shape[1])]
        + [_const_spec(a.shape) for a in wts],
        out_specs=(row(d), pl.BlockSpec((d // 2 // PLANE_WIDTH, tm, PLANE_WIDTH), lambda i: (0, i, 0)),
                   row(TOP_K), row(TOP_K)),
        out_shape=(jax.ShapeDtypeStruct((n, d), F32),
                   jax.ShapeDtypeStruct((d // 2 // PLANE_WIDTH, n, PLANE_WIDTH), jnp.int32),
                   jax.ShapeDtypeStruct((n, TOP_K), jnp.int32), jax.ShapeDtypeStruct((n, TOP_K), F32)),
        compiler_params=_cparams(("parallel",), 40), name="merge_router")(x, oa, ub, hc, gates, *wts)


def _moe_body(te_ref, gs_ref, tr_ref, nx_ref, sl_ref, xs_ref, wgu_hbm, bgu_ref, wd_hbm, bd_ref, ys_ref,
              wgu_buf, wd_buf, wgub_ref, wdb_ref, sem, *, layer):
    i = pl.program_id(0)
    n_planes, tm, pw = xs_ref.shape

    def weight_copies(expert, slot):
        return (pltpu.make_async_copy(wgu_hbm.at[layer, expert], wgu_buf.at[slot], sem.at[0, slot]),
                pltpu.make_async_copy(wd_hbm.at[layer, expert], wd_buf.at[slot], sem.at[1, slot]))

    @pl.when(i == 0)
    def _():
        for c in weight_copies(te_ref[0], 0):
            c.start()

    @pl.when(gs_ref[i] == 1)
    def _():
        slot = sl_ref[i]
        for c in weight_copies(te_ref[i], slot):
            c.wait()

        @pl.when(nx_ref[i] >= 0)
        def _():
            for c in weight_copies(nx_ref[i], 1 - slot):
                c.start()

        wgub_ref[...] = wgu_buf[slot].astype(BF16)
        wdb_ref[...] = wd_buf[slot].astype(BF16)

    @pl.when(tr_ref[i] > 0)
    def _():
        dff = wdb_ref.shape[0]
        live = lax.broadcasted_iota(jnp.int32, (tm, pw), 0) < tr_ref[i]
        half = n_planes * pw
        gu = bgu_ref[...]
        for p in range(n_planes):
            lo, hi = _unpack_halves(jnp.where(live, xs_ref[p], 0))
            gu = gu + _dot(lo.astype(BF16), wgub_ref[p * pw:(p + 1) * pw, :])
            gu = gu + _dot(hi.astype(BF16), wgub_ref[half + p * pw:half + (p + 1) * pw, :])
        gate = jnp.minimum(gu[:, :dff], SWIGLU_LIMIT)
        up = jnp.clip(gu[:, dff:], -SWIGLU_LIMIT, SWIGLU_LIMIT)
        hid = (up + 1.0) * gate * jax.nn.sigmoid(SWIGLU_ALPHA * gate)
        words = _pack_halves(_dot(hid.astype(BF16), wdb_ref[...]) + bd_ref[...])
        for p in range(n_planes):
            ys_ref[p] = words[:, p * pw:(p + 1) * pw]

    @pl.when(tr_ref[i] == 0)
    def _():
        ys_ref[...] = jnp.zeros_like(ys_ref)


def _moe_experts(layer, tables, xs, w_gate_up, b_gate_up, w_down, b_down):
    n_planes, r, pw = xs.shape
    d = 2 * n_planes * pw
    tm = MOE_TILE
    n_exp, _, dgu = w_gate_up.shape[1:]
    dff = w_down.shape[2]
    bgu = b_gate_up.reshape(-1, 1, dgu)
    bd = b_down.reshape(-1, 1, d)
    grid_spec = pltpu.PrefetchScalarGridSpec(
        num_scalar_prefetch=len(tables), grid=(r // tm,),
        in_specs=[pl.BlockSpec((n_planes, tm, pw), lambda i, te, *_: (0, i, 0)),
                  pl.BlockSpec(memory_space=pl.ANY),
                  pl.BlockSpec((None, 1, dgu), lambda i, te, *_: (layer * n_exp + te[i], 0, 0)),
                  pl.BlockSpec(memory_space=pl.ANY),
                  pl.BlockSpec((None, 1, d), lambda i, te, *_: (layer * n_exp + te[i], 0, 0))],
        out_specs=pl.BlockSpec((n_planes, tm, pw), lambda i, te, *_: (0, i, 0)),
        scratch_shapes=[pltpu.VMEM((2, d, dgu), F32), pltpu.VMEM((2, dff, d), F32),
                        pltpu.VMEM((d, dgu), BF16), pltpu.VMEM((dff, d), BF16),
                        pltpu.SemaphoreType.DMA((2, 2))])
    return pl.pallas_call(
        functools.partial(_moe_body, layer=layer), grid_spec=grid_spec,
        out_shape=jax.ShapeDtypeStruct((n_planes, r, pw), jnp.int32),
        compiler_params=_cparams(("arbitrary",), 56), name="moe_experts")(
            *tables, xs, w_gate_up, bgu, w_down, bd)


def _combine_body(x1_ref, yg_ref, tw_ref, g_ref, b_ref, o_ref, *, alpha):
    tw = tw_ref[...]
    n_planes = yg_ref.shape[1]
    lows, highs = [], []
    for p in range(n_planes):
        lo_acc = hi_acc = None
        for k in range(TOP_K):
            lo, hi = _unpack_halves(yg_ref[k, p])
            wk = tw[:, k:k + 1]
            lo_acc = wk * lo if lo_acc is None else lo_acc + wk * lo
            hi_acc = wk * hi if hi_acc is None else hi_acc + wk * hi
        lows.append(lo_acc)
        highs.append(hi_acc)
    y = jnp.concatenate(lows + highs, axis=1)
    o_ref[...] = _layer_norm(alpha * x1_ref[...] + y, g_ref[...], b_ref[...])


def _combine(x1, yg, tw, g, b, alpha):
    n, d = x1.shape
    n_planes, pw = yg.shape[1], yg.shape[3]
    tm = TOKEN_TILE
    return pl.pallas_call(
        functools.partial(_combine_body, alpha=alpha), grid=(n // tm,),
        in_specs=[pl.BlockSpec((tm, d), lambda i: (i, 0)),
                  pl.BlockSpec((TOP_K, n_planes, tm, pw), lambda i: (0, 0, i, 0)),
                  pl.BlockSpec((tm, TOP_K), lambda i: (i, 0)), _const_spec(g.shape), _const_spec(b.shape)],
        out_specs=pl.BlockSpec((tm, d), lambda i: (i, 0)), out_shape=jax.ShapeDtypeStruct((n, d), F32),
        compiler_params=_cparams(("parallel",), 32), name="moe_combine_norm")(x1, yg, tw, g, b)


def _dispatch_body(idx_ref, pos_ref, meta_ref, cnt_ref, base_ref):
    ph = pl.program_id(0)
    i = pl.program_id(1)
    tb = idx_ref.shape[0]
    lane = lax.broadcasted_iota(jnp.int32, (tb, LANES), 1)
    idx = idx_ref[...]
    onehot = [(idx[:, k:k + 1] == lane).astype(F32) for k in range(TOP_K)]
    colsum = [jnp.sum(o, axis=0, keepdims=True) for o in onehot]

    @pl.when(jnp.logical_and(ph == 0, i == 0))
    def _():
        cnt_ref[...] = jnp.zeros_like(cnt_ref)

    @pl.when(ph == 0)
    def _():
        cnt_ref[...] = cnt_ref[...] + (colsum[0] + colsum[1] + colsum[2] + colsum[3])

    @pl.when(jnp.logical_and(ph == 1, i == 0))
    def _():
        cnt = cnt_ref[...]
        ptiles = jnp.floor((cnt + (MOE_TILE - 1)) * (1.0 / MOE_TILE))
        tile_end = _lane_cumsum(ptiles)
        base_ref[...] = (tile_end - ptiles) * MOE_TILE
        row = lax.broadcasted_iota(jnp.int32, meta_ref.shape, 0)
        meta_ref[...] = jnp.where(row == 0, tile_end[0:1, :], jnp.where(row == 1, cnt[0:1, :], 0.0)).astype(jnp.int32)

    @pl.when(ph == 1)
    def _():
        earlier = (lax.broadcasted_iota(jnp.int32, (tb, tb), 0) > lax.broadcasted_iota(jnp.int32, (tb, tb), 1))
        earlier = earlier.astype(BF16)
        base = base_ref[0:1, :]
        slot = lax.broadcasted_iota(jnp.int32, (tb, TOP_K), 1)
        pos = jnp.zeros((tb, TOP_K), F32)
        for k in range(TOP_K):
            seen = _dot(earlier, onehot[k].astype(BF16)) + base
            pos = jnp.where(slot == k, jnp.sum(onehot[k] * seen, axis=1, keepdims=True), pos)
            base = base + colsum[k]
        base_ref[...] = jnp.broadcast_to(base, base_ref.shape)
        pos_ref[...] = pos.astype(jnp.int32)


def _dispatch(top_idx):
    n = top_idx.shape[0]
    tm = MOE_TILE
    n_tiles = -(-(n * TOP_K) // tm) + N_EXPERTS
    tb = next(t for t in DISPATCH_TILES if n % t == 0)
    pos, meta = pl.pallas_call(
        _dispatch_body, grid=(2, n // tb),
        in_specs=[pl.BlockSpec((tb, TOP_K), lambda ph, i: (i, 0))],
        out_specs=(pl.BlockSpec((tb, TOP_K), lambda ph, i: (i * ph, 0)), _const_spec((SUBLANES, LANES))),
        out_shape=(jax.ShapeDtypeStruct((n, TOP_K), jnp.int32), jax.ShapeDtypeStruct((SUBLANES, LANES), jnp.int32)),
        scratch_shapes=[pltpu.VMEM((SUBLANES, LANES), F32), pltpu.VMEM((SUBLANES, LANES), F32)],
        compiler_params=_cparams(("arbitrary", "arbitrary"), 32), name="moe_dispatch")(top_idx)
    tile_end, cnt = meta[0, :N_EXPERTS], meta[1, :N_EXPERTS]
    experts = jnp.arange(N_EXPERTS, dtype=jnp.int32)
    tiles = jnp.arange(n_tiles, dtype=jnp.int32)
    te = jnp.minimum(jnp.sum((tiles[:, None] >= tile_end[None, :]).astype(jnp.int32), axis=1), N_EXPERTS - 1)
    of_tile = (te[:, None] == experts[None, :]).astype(jnp.int32)

    def per_tile(v):
        return jnp.sum(of_tile * v[None, :], axis=1)

    tile_start = tile_end - (cnt + tm - 1) // tm
    tr = jnp.clip(per_tile(cnt) - (tiles - per_tile(tile_start)) * tm, 0, tm)
    tr = tr * (tiles < tile_end[-1]).astype(jnp.int32)
    gs = jnp.logical_and(tiles == per_tile(tile_start), tr > 0).astype(jnp.int32)
    later = jnp.logical_and(experts[None, :] > experts[:, None], cnt[None, :] > 0)
    nxt = jnp.min(jnp.where(later, experts[None, :], N_EXPERTS), axis=1)
    nxt = jnp.where(nxt == N_EXPERTS, -1, nxt)
    order = jnp.sum(jnp.logical_and(experts[None, :] < experts[:, None], cnt[None, :] > 0).astype(jnp.int32), axis=1)
    return pos, (te, gs, tr, per_tile(nxt), per_tile(order % 2)), n_tiles * tm


def _sc_mesh():
    return plsc.VectorSubcoreMesh(core_axis_name="c", subcore_axis_name="s")


def _sc_gather_rows(table, idx):
    m = idx.shape[0]
    w = table.shape[1]

    @functools.partial(pl.kernel, out_type=jax.ShapeDtypeStruct((m, w), table.dtype), mesh=_sc_mesh())
    def gather(t_hbm, i_hbm, o_hbm):
        def body(i_vmem, o_vmem):
            pltpu.sync_copy(t_hbm.at[i_vmem.at[0]], o_vmem)

        pltpu.emit_pipeline(
            body, grid=(m // SC_WINDOW,),
            in_specs=[pl.BlockSpec((1, SC_WINDOW), index_map=lambda i: (0, i))],
            out_specs=[pl.BlockSpec((SC_WINDOW, w), index_map=lambda i: (i, 0))],
            core_axis_name=("c", "s"), dimension_semantics=(pltpu.PARALLEL,))(i_hbm, o_hbm)

    return gather(table, idx.reshape(1, m))


def _sc_scatter_rows(rows, idx, n_out):
    n_src, w = rows.shape
    m = idx.shape[0]
    src_blocks = n_src // SC_WINDOW

    @functools.partial(pl.kernel, out_type=jax.ShapeDtypeStruct((n_out, w), rows.dtype), mesh=_sc_mesh(),
                       scratch_types=[])
    def scatter(x_hbm, i_hbm, o_hbm):
        def body(x_vmem, i_vmem):
            pltpu.sync_copy(x_vmem, o_hbm.at[i_vmem.at[0]])

        pltpu.emit_pipeline(
            body, grid=(m // SC_WINDOW,),
            in_specs=[pl.BlockSpec((SC_WINDOW, w), index_map=lambda i: (lax.rem(i, src_blocks), 0)),
                      pl.BlockSpec((1, SC_WINDOW), index_map=lambda i: (0, i))],
            out_specs=[], core_axis_name=("c", "s"), dimension_semantics=(pltpu.PARALLEL,))(x_hbm, i_hbm)

    return scatter(rows, idx.reshape(1, m))


def _plane_rows(pos, n_planes, rows_per_plane):
    off = jnp.arange(n_planes, dtype=jnp.int32) * rows_per_plane
    return (pos.T[:, None, :] + off[None, :, None]).reshape(-1)


def _pad_rows(a, rows):
    return jnp.pad(a, ((0, 0), (rows - a.shape[1], 0), (0, 0)))


def _block_diag(w):
    h, bd, _ = w.shape
    eye = jnp.eye(h, dtype=w.dtype)
    return jnp.einsum("hij,hg->higj", w, eye).reshape(h * bd, h * bd)


def kernel(x_prompt, x_sample, cache_k, cache_v, cache_logf, state_conv_b, state_conv_c, state_h, page_table,
           w_in, b_f, w_att_o, conv_b_w, conv_b_b, lnb_g, lnb_b, w_conv_o, b_conv_o, conv_c_w, conv_c_b,
           w_rg_a, b_rg_a, w_rg_x, b_rg_x, rg_lambda, w_rg_o, w_out, ln1_g, ln1_b, w_router, b_router,
           w_gate_up, b_gate_up, w_down, b_down, ln2_g, ln2_b):
    bp, seq, d = x_prompt.shape
    bs, ns, _ = x_sample.shape
    depth, n_pool, page = cache_k.shape[:3]
    w_conv = state_conv_b.shape[-1]
    w_rg = state_conv_c.shape[-1]
    alpha = (2 * depth) ** 0.25
    n_p, n_s = bp * seq, bs * ns
    off_f = 3 * W_ATT
    off_glu = off_f + H_ATT
    off_rg = off_glu + 2 * w_conv
    off_gate = off_rg + 2 * w_rg

    ckt = cache_k.transpose(0, 1, 3, 4, 2).reshape(depth, n_pool, W_ATT, page)
    cvt = cache_v.transpose(0, 1, 3, 4, 2).reshape(depth, n_pool, W_ATT, page)
    cft = cache_logf.transpose(0, 1, 3, 2)
    head_eye = jnp.eye(H_ATT, dtype=BF16)

    xp = x_prompt.reshape(n_p, d)
    xs = x_sample.reshape(n_s, d)
    outs = {k: [] for k in ("ks", "vs", "fs", "cbp", "cbs", "ccp", "ccs", "hp", "hs")}
    stacked = None

    for l in range(depth):
        wl = w_in[l]
        wf = wl[:, off_f:off_glu]
        in_w = (wl[:, :2 * W_ATT].astype(BF16), wl[:, :off_f].T.astype(BF16),
                jnp.pad(wf, ((0, 0), (0, LANES - H_ATT))).astype(BF16), wf.T.astype(BF16),
                b_f[l].reshape(1, H_ATT), b_f[l].reshape(H_ATT, 1),
                wl[:, off_glu:off_rg].astype(BF16), wl[:, off_rg:off_gate].astype(BF16),
                wl[:, off_gate:].astype(BF16))
        cbw = jnp.broadcast_to(conv_b_w[l][:, None, :], (CONV_WIDTH, SUBLANES, w_conv))
        ccw = jnp.pad(conv_c_w[l], ((0, CONVC_HALO - RG_CONV_WIDTH), (0, 0)))
        rg_w = (ccw, conv_c_b[l].reshape(1, -1), _block_diag(w_rg_a[l]).astype(BF16), b_rg_a[l].reshape(1, -1),
                _block_diag(w_rg_x[l]).astype(BF16), b_rg_x[l].reshape(1, -1), rg_lambda[l].reshape(1, -1))
        cb_w = (cbw, conv_b_b[l].reshape(1, -1), lnb_g[l].reshape(1, -1), lnb_b[l].reshape(1, -1))
        wr = jnp.pad(w_router[l], ((0, 0), (0, LANES - N_EXPERTS)))
        br = jnp.pad(b_router[l], (0, LANES - N_EXPERTS), constant_values=NEG_BIG).reshape(1, LANES)
        mg_w = (w_att_o[l].astype(BF16), w_conv_o[l].astype(BF16), b_conv_o[l].reshape(1, -1),
                w_rg_o[l].astype(BF16), w_out[l].astype(BF16), ln1_g[l].reshape(1, -1), ln1_b[l].reshape(1, -1),
                wr.astype(BF16), (wr - wr.astype(BF16).astype(F32)).astype(BF16), br)

        (q, kb, qtm, kt, vt, vtb, lf, lft, u, rgx, gg, gates) = _inproj(xp, in_w, seq, l, depth, stacked)
        stacked = (kt, vt, lft)
        oa = _fox_prompt(qtm, kb, vtb, _cumsum(lf, seq), bp, seq)
        u3 = u.reshape(bp, seq, w_conv)
        ub = _convb(u3, jnp.zeros((bp, CONVB_HALO, w_conv), F32), *cb_w)
        rgx3 = rgx.reshape(bp, seq, w_rg)
        hc, hl = _rglru(rgx3, gg.reshape(bp, seq, w_rg), jnp.zeros((bp, CONVC_HALO, w_rg), F32),
                        jnp.zeros((bp, 1, w_rg), F32), *rg_w)
        x1_p, x1p, tip, twp = _merge(xp, oa, ub.reshape(n_p, w_conv), hc.reshape(n_p, w_rg), gates, mg_w, alpha)
        outs["cbp"].append(u3[:, seq - (CONV_WIDTH - 1):])
        outs["ccp"].append(rgx3[:, seq - (RG_CONV_WIDTH - 1):])
        outs["hp"].append(hl.reshape(bp, w_rg))

        (q, kb, qtm, kt, vt, vtb, lf, lft, u, rgx, gg, gates) = _inproj(xs, in_w, n_s)
        qbd = jnp.einsum("bthd,hg->bhtgd", q.reshape(bs, ns, H_ATT, HD_ATT), head_eye)
        qbd = qbd.reshape(bs, H_ATT * ns, W_ATT)

        def new_page(a):
            a = a.reshape(a.shape[0], bs, ns).transpose(1, 0, 2)
            return jnp.pad(a, ((0, 0), (0, 0), (0, page - ns)))

        kt, vt, lft = kt[0, 0], vt[0, 0], lft[0, 0]
        oa = _fox_sample(l, page_table, qbd, ckt, cvt, cft, new_page(kt), new_page(vt), new_page(lft), ns)
        u3 = u.reshape(bs, ns, w_conv)
        ub = _convb(u3, _pad_rows(state_conv_b[l], CONVB_HALO), *cb_w)
        rgx3 = rgx.reshape(bs, ns, w_rg)
        hc, hl = _rglru(rgx3, gg.reshape(bs, ns, w_rg), _pad_rows(state_conv_c[l], CONVC_HALO),
                        state_h[l].reshape(bs, 1, w_rg), *rg_w)
        x1_s, x1s, tis, tws = _merge(xs, oa.reshape(n_s, W_ATT), ub.reshape(n_s, w_conv), hc.reshape(n_s, w_rg),
                                     gates, mg_w, alpha)
        outs["ks"].append(kt.T.reshape(bs, ns, H_ATT, HD_ATT))
        outs["vs"].append(vt.T.reshape(bs, ns, H_ATT, HD_ATT))
        outs["fs"].append(lf.reshape(bs, ns, H_ATT))
        outs["cbs"].append(jnp.concatenate([state_conv_b[l], u3], axis=1)[:, ns:])
        outs["ccs"].append(jnp.concatenate([state_conv_c[l], rgx3], axis=1)[:, ns:])
        outs["hs"].append(hl.reshape(bs, w_rg))

        pos, tables, n_rows = _dispatch(jnp.concatenate([tip, tis], axis=0))
        x1_all = jnp.concatenate([x1p, x1s], axis=1)
        n_planes, n_all, pw = x1_all.shape
        xs_rows = _sc_scatter_rows(x1_all.reshape(n_planes * n_all, pw), _plane_rows(pos, n_planes, n_rows),
                                   n_planes * n_rows)
        ys = _moe_experts(l, tables, xs_rows.reshape(n_planes, n_rows, pw), w_gate_up, b_gate_up, w_down, b_down)
        ys = ys.reshape(n_planes * n_rows, pw)
        g2, b2 = ln2_g[l].reshape(1, -1), ln2_b[l].reshape(1, -1)
        yg = _sc_gather_rows(ys, _plane_rows(pos[:n_p], n_planes, n_rows)).reshape(TOP_K, n_planes, n_p, pw)
        xp = _combine(x1_p, yg, twp, g2, b2, alpha)
        yg = _sc_gather_rows(ys, _plane_rows(pos[n_p:], n_planes, n_rows)).reshape(TOP_K, n_planes, n_s, pw)
        xs = _combine(x1_s, yg, tws, g2, b2, alpha)

    st = {k: jnp.stack(v, 0) for k, v in outs.items()}
    kt, vt, lft = stacked
    st["kp"] = kt.reshape(depth, bp, H_ATT, HD_ATT, seq).transpose(0, 1, 4, 2, 3)
    st["vp"] = vt.reshape(depth, bp, H_ATT, HD_ATT, seq).transpose(0, 1, 4, 2, 3)
    st["fp"] = lft.transpose(0, 1, 3, 2)
    return (xp.reshape(bp, seq, d), xs.reshape(bs, ns, d), st["kp"], st["vp"], st["fp"], st["ks"], st["vs"],
            st["fs"], st["cbp"], st["cbs"], st["ccp"], st["ccs"], st["hp"], st["hs"])
```

```python
import functools
import math

import jax
import jax.numpy as jnp
from jax import lax
from jax.experimental import pallas as pl
from jax.experimental.pallas import tpu as pltpu
from jax.experimental.pallas import tpu_sc as plsc

F32 = jnp.float32
BF16 = jnp.bfloat16

H_ATT = 8
HD_ATT = 64
W_ATT = H_ATT * HD_ATT
CONV_WIDTH = 31
RG_CONV_WIDTH = 4
H_RG = 8
RG_C = 8.0
N_EXPERTS = 32
TOP_K = 4
SWIGLU_LIMIT = 7.0
SWIGLU_ALPHA = 1.702
LN_EPS = 1e-5
N_BRANCH = 3

LANES = 128
SUBLANES = 8
NEG_BIG = -1e30

TOKEN_TILE = 256
MERGE_TILE = 512
ATT_TILE = 512
CUMSUM_TILE = 512
SEQ_TILE = 256
CONVB_HALO = 32
CONVC_HALO = 8
CONV_CHUNK = 32
PAGES_PER_STEP = 32
MOE_TILE = 256
PLANE_WIDTH = 256
SC_WINDOW = 128
DISPATCH_TILES = (1280, 1024, 768, 512, 256)


def _cparams(sem, vmem_mb):
    return pltpu.CompilerParams(dimension_semantics=sem, vmem_limit_bytes=vmem_mb << 20)


def _const_spec(shape):
    nd = len(shape)
    return pl.BlockSpec(shape, lambda *_: (0,) * nd)


def _log_sigmoid(z):
    return jnp.minimum(z, 0.0) - jnp.log1p(jnp.exp(-jnp.abs(z)))


def _softplus(z):
    return jnp.maximum(z, 0.0) + jnp.log1p(jnp.exp(-jnp.abs(z)))


def _layer_norm(x, g, b):
    mu = jnp.mean(x, axis=-1, keepdims=True)
    xc = x - mu
    var = jnp.mean(xc * xc, axis=-1, keepdims=True)
    return xc * lax.rsqrt(var + LN_EPS) * g + b


def _pack_halves(x):
    n = x.shape[1] // 2
    lo = lax.bitcast_convert_type(x[:, :n].astype(BF16).astype(F32), jnp.uint32)
    hi = lax.bitcast_convert_type(x[:, n:].astype(BF16).astype(F32), jnp.uint32)
    return lax.bitcast_convert_type(hi | (lo >> 16), jnp.int32)


def _unpack_halves(w):
    u = lax.bitcast_convert_type(w, jnp.uint32)
    return (lax.bitcast_convert_type(u << 16, F32),
            lax.bitcast_convert_type(u & jnp.uint32(0xFFFF0000), F32))


def _dot(a, b):
    return jnp.dot(a, b, preferred_element_type=F32)


def _dot_nt(a, b, precision=None):
    return lax.dot_general(a, b, (((1,), (1,)), ((), ())), preferred_element_type=F32, precision=precision)


def _store_slab(ref, layer, val):
    if len(ref.shape) == val.ndim:
        ref[...] = val
    else:
        for l in range(ref.shape[0]):
            ref[l] = val if l == layer else jnp.zeros_like(val)


def _inproj_body(x_ref, wqk_ref, wt_ref, wf_ref, wft_ref, bf_ref, bft_ref, wglu_ref, wrg_ref, wgate_ref,
                 q_ref, kb_ref, qtm_ref, kt_ref, vt_ref, vtb_ref, lf_ref, lft_ref, u_ref, rgx_ref, gg_ref, gates_ref,
                 *, layer):
    xb = x_ref[...].astype(BF16)
    w = W_ATT
    scale = HD_ATT ** -0.5
    qk = _dot(xb, wqk_ref[...])
    q_ref[...] = (qk[:, :w] * scale).astype(BF16)
    kb_ref[...] = qk[:, w:].astype(BF16)
    t = _dot_nt(wt_ref[...], xb)
    _store_slab(kt_ref, layer, t[w:2 * w, :])
    vt = t[2 * w:, :]
    _store_slab(vt_ref, layer, vt)
    vtb_ref[...] = vt.astype(BF16)
    qt = (t[:w, :] * scale).astype(BF16)
    pair = 2 * HD_ATT
    lower = lax.broadcasted_iota(jnp.int32, (pair, qt.shape[1]), 0) < HD_ATT
    for h in range(H_ATT):
        slab = qt[(h // 2) * pair:(h // 2 + 1) * pair, :]
        qtm_ref[h] = jnp.where(lower if h % 2 == 0 else jnp.logical_not(lower), slab, jnp.zeros_like(slab))
    zf = _dot(xb, wf_ref[...])
    lf_ref[...] = _log_sigmoid(zf[:, :H_ATT] + bf_ref[...])
    _store_slab(lft_ref, layer, _log_sigmoid(_dot_nt(wft_ref[...], xb) + bft_ref[...]))
    glu = _dot(xb, wglu_ref[...])
    c = glu.shape[1] // 2
    u_ref[...] = glu[:, :c] * jax.nn.sigmoid(glu[:, c:])
    rg = _dot(xb, wrg_ref[...])
    c = rg.shape[1] // 2
    rgx_ref[...] = rg[:, :c]
    gg_ref[...] = jax.nn.gelu(rg[:, c:]).astype(gg_ref.dtype)
    d = gates_ref.shape[1] // N_BRANCH
    for j in range(N_BRANCH):
        gate = jax.nn.sigmoid(_dot(xb, wgate_ref[:, j * d:(j + 1) * d]))
        gates_ref[:, j * d:(j + 1) * d] = gate.astype(gates_ref.dtype)


_INPROJ_STACKED_OUTS = (3, 4, 7)


def _inproj_carry_body(*refs, n_in, layer):
    _inproj_body(*refs[:n_in], *refs[n_in + len(_INPROJ_STACKED_OUTS):], layer=layer)


def _inproj(x, wts, seq, layer=0, depth=1, carried=None):
    n, d = x.shape
    tm = TOKEN_TILE
    wglu, wrg, wgate = wts[-3:]
    wc, wr = wglu.shape[1] // 2, wrg.shape[1] // 2
    bps = seq // tm

    def row(c):
        return pl.BlockSpec((tm, c), lambda i: (i, 0))

    def col(r):
        return pl.BlockSpec((None, r, tm), lambda i: (i // bps, 0, i % bps))

    def slab(r):
        if carried is None:
            return pl.BlockSpec((depth, None, r, tm), lambda i: (0, i // bps, 0, i % bps))
        return pl.BlockSpec((None, None, r, tm), lambda i: (layer, i // bps, 0, i % bps))

    def slabshape(r):
        return jax.ShapeDtypeStruct((depth, n // seq, r, seq), F32)

    out_shape = (
        jax.ShapeDtypeStruct((n, W_ATT), BF16), jax.ShapeDtypeStruct((n, W_ATT), BF16),
        jax.ShapeDtypeStruct((H_ATT, 2 * HD_ATT, n), BF16),
        slabshape(W_ATT), slabshape(W_ATT), jax.ShapeDtypeStruct((n // seq, W_ATT, seq), BF16),
        jax.ShapeDtypeStruct((n, H_ATT), F32), slabshape(H_ATT), jax.ShapeDtypeStruct((n, wc), F32),
        jax.ShapeDtypeStruct((n, wr), F32), jax.ShapeDtypeStruct((n, wr), BF16),
        jax.ShapeDtypeStruct((n, wgate.shape[1]), BF16))
    out_specs = (row(W_ATT), row(W_ATT), pl.BlockSpec((H_ATT, 2 * HD_ATT, tm), lambda i: (0, 0, i)),
                 slab(W_ATT), slab(W_ATT), col(W_ATT),
                 row(H_ATT), slab(H_ATT), row(wc), row(wr), row(wr), row(wgate.shape[1]))
    in_specs = [row(d)] + [_const_spec(a.shape) for a in wts]
    args = [x, *wts]
    if carried is None:
        body, aliases = functools.partial(_inproj_body, layer=layer), {}
    else:
        body = functools.partial(_inproj_carry_body, n_in=len(args), layer=layer)
        aliases = {len(args) + j: o for j, o in enumerate(_INPROJ_STACKED_OUTS)}
        in_specs += [pl.BlockSpec(memory_space=pl.ANY)] * len(carried)
        args += list(carried)
    return pl.pallas_call(
        body, grid=(n // tm,), in_specs=in_specs, out_specs=out_specs, out_shape=out_shape,
        input_output_aliases=aliases,
        compiler_params=_cparams(("parallel",), 56), name="inproj")(*args)


def _lane_cumsum(c):
    n = c.shape[1]
    lane = lax.broadcasted_iota(jnp.int32, c.shape, 1)
    d = 1
    while d < n:
        c = c + jnp.where(lane >= d, pltpu.roll(c, d, axis=1), 0.0)
        d *= 2
    return c


def _cumsum_body(f_ref, c_ref, carry_ref, *, blocks_per_seq):
    @pl.when(pl.program_id(0) % blocks_per_seq == 0)
    def _():
        carry_ref[...] = jnp.zeros_like(carry_ref)

    c = f_ref[...]
    n = c.shape[0]
    row = lax.broadcasted_iota(jnp.int32, c.shape, 0)
    d = 1
    while d < n:
        c = c + jnp.where(row >= d, pltpu.roll(c, d, axis=0), 0.0)
        d *= 2
    c = c + carry_ref[0:1, :]
    carry_ref[...] = jnp.broadcast_to(c[n - 1:n, :], carry_ref.shape)
    for h in range(H_ATT):
        c_ref[h] = jnp.broadcast_to(c[:, h:h + 1], (n, LANES))


def _cumsum(lf, seq):
    n, h = lf.shape
    tc = min(CUMSUM_TILE, seq)
    return pl.pallas_call(
        functools.partial(_cumsum_body, blocks_per_seq=seq // tc), grid=(n // tc,),
        in_specs=[pl.BlockSpec((tc, h), lambda i: (i, 0))],
        out_specs=pl.BlockSpec((h, tc, LANES), lambda i: (0, i, 0)),
        out_shape=jax.ShapeDtypeStruct((h, n, LANES), F32),
        scratch_shapes=[pltpu.VMEM((SUBLANES, h), F32)],
        compiler_params=_cparams(("arbitrary",), 16), name="logf_cumsum")(lf)


def _fox_prompt_body(qi_ref, ki_ref, qtm_ref, k_ref, vt_ref, ck_ref, cq_ref, o_ref, m_ref, l_ref, acc_ref):
    qi = qi_ref[pl.program_id(1)]
    ki = ki_ref[pl.program_id(1)]
    tk, tq = k_ref.shape[0], qtm_ref.shape[2]
    pair = 2 * HD_ATT

    @pl.when(ki == 0)
    def _():
        m_ref[...] = jnp.full_like(m_ref, NEG_BIG)
        l_ref[...] = jnp.zeros_like(l_ref)
        acc_ref[...] = jnp.zeros_like(acc_ref)

    def update(diagonal):
        if diagonal:
            keep = lax.broadcasted_iota(jnp.int32, (tk, tq), 0) <= lax.broadcasted_iota(jnp.int32, (tk, tq), 1)
        for h in range(H_ATT):
            s = _dot(k_ref[:, (h // 2) * pair:(h // 2 + 1) * pair], qtm_ref[h])
            bias = ck_ref[h] - cq_ref[h, 0:1, :]
            s = s - jnp.concatenate([bias] * (tq // LANES), axis=1)
            if diagonal:
                s = jnp.where(keep, s, NEG_BIG)
            m_old = m_ref[h]
            m_new = jnp.maximum(m_old, jnp.max(s, axis=0, keepdims=True))
            p = jnp.exp(s - m_new)
            alpha = jnp.exp(m_old - m_new)
            l_ref[h] = alpha * l_ref[h] + jnp.sum(p, axis=0, keepdims=True)
            acc_ref[h] = alpha * acc_ref[h] + _dot(vt_ref[h * HD_ATT:(h + 1) * HD_ATT, :], p.astype(BF16))
            m_ref[h] = m_new

    @pl.when(ki < qi)
    def _():
        update(False)

    @pl.when(ki == qi)
    def _():
        update(True)
        for j in range(H_ATT // 2):
            o2 = jnp.concatenate([acc_ref[2 * j] / l_ref[2 * j], acc_ref[2 * j + 1] / l_ref[2 * j + 1]], axis=0)
            o_ref[:, j * pair:(j + 1) * pair] = o2.T.astype(o_ref.dtype)


def _fox_prompt(qtm, kb, vt, cb, batch, seq):
    n = kb.shape[0]
    t = min(ATT_TILE, seq)
    nb = seq // t
    pairs = [(qi, ki) for qi in range(nb) for ki in range(qi + 1)]
    qi_of = jnp.array([p[0] for p in pairs], jnp.int32)
    ki_of = jnp.array([p[1] for p in pairs], jnp.int32)
    grid_spec = pltpu.PrefetchScalarGridSpec(
        num_scalar_prefetch=2, grid=(batch, len(pairs)),
        in_specs=[pl.BlockSpec((H_ATT, 2 * HD_ATT, t), lambda b, p, qs, ks: (0, 0, b * nb + qs[p])),
                  pl.BlockSpec((t, W_ATT), lambda b, p, qs, ks: (b * nb + ks[p], 0)),
                  pl.BlockSpec((None, W_ATT, t), lambda b, p, qs, ks: (b, 0, ks[p])),
                  pl.BlockSpec((H_ATT, t, LANES), lambda b, p, qs, ks: (0, b * nb + ks[p], 0)),
                  pl.BlockSpec((H_ATT, SUBLANES, LANES),
                               lambda b, p, qs, ks: (0, (b * nb + qs[p]) * (t // SUBLANES), 0))],
        out_specs=pl.BlockSpec((t, W_ATT), lambda b, p, qs, ks: (b * nb + qs[p], 0)),
        scratch_shapes=[pltpu.VMEM((H_ATT, 1, t), F32), pltpu.VMEM((H_ATT, 1, t), F32),
                        pltpu.VMEM((H_ATT, HD_ATT, t), F32)])
    return pl.pallas_call(
        _fox_prompt_body, grid_spec=grid_spec, out_shape=jax.ShapeDtypeStruct((n, W_ATT), BF16),
        compiler_params=_cparams(("parallel", "arbitrary"), 40), name="fox_prompt")(
            qi_of, ki_of, qtm, kb, vt, cb, cb)


def _fox_sample_body(pt_ref, qbd_ref, *refs, n_pages_step, n_new):
    del pt_ref
    np_ = n_pages_step
    k_refs = refs[:np_]
    v_refs = refs[np_:2 * np_]
    f_refs = refs[2 * np_:3 * np_]
    kn_ref, vn_ref, fn_ref, o_ref, m_ref, l_ref, acc_ref, carry_ref, kb_ref, vb_ref = refs[3 * np_:]
    c = pl.program_id(1)
    rows = qbd_ref.shape[0]
    page = kn_ref.shape[1]

    @pl.when(c == 0)
    def _():
        m_ref[...] = jnp.full_like(m_ref, NEG_BIG)
        l_ref[...] = jnp.zeros_like(l_ref)
        acc_ref[...] = jnp.zeros_like(acc_ref)
        carry_ref[...] = jnp.zeros_like(carry_ref)

    def decay_rows(ft):
        ct = _lane_cumsum(ft) + carry_ref[:, 0:1]
        carry_ref[...] = jnp.broadcast_to(ct[:, ct.shape[1] - 1:], carry_ref.shape)
        return jnp.concatenate([jnp.broadcast_to(ct[h:h + 1, :], (n_new, ct.shape[1])) for h in range(H_ATT)], axis=0)

    def update(kb, vb, cexp, keep):
        s = _dot(qbd_ref[...], kb) - cexp
        if keep is not None:
            s = jnp.where(keep, s, NEG_BIG)
        m_old = m_ref[...]
        m_new = jnp.maximum(m_old, jnp.max(s, axis=-1, keepdims=True))
        p = jnp.exp(s - m_new)
        alpha = jnp.exp(m_old - m_new)
        l_ref[...] = alpha * l_ref[...] + jnp.sum(p, axis=-1, keepdims=True)
        acc_ref[...] = alpha * acc_ref[...] + _dot_nt(p.astype(BF16), vb)
        m_ref[...] = m_new

    for j in range(np_):
        kb_ref[:, j * page:(j + 1) * page] = k_refs[j][...].astype(BF16)
        vb_ref[:, j * page:(j + 1) * page] = v_refs[j][...].astype(BF16)
    ft = jnp.concatenate([r[...] for r in f_refs], axis=1)
    update(kb_ref[...], vb_ref[...], decay_rows(ft), None)

    @pl.when(c == pl.num_programs(1) - 1)
    def _():
        tok = lax.broadcasted_iota(jnp.int32, (rows, page), 0) % n_new
        key = lax.broadcasted_iota(jnp.int32, (rows, page), 1)
        update(kn_ref[...].astype(BF16), vn_ref[...].astype(BF16), decay_rows(fn_ref[...]), key <= tok)
        acc = acc_ref[...] / l_ref[...]
        for h in range(H_ATT):
            o_ref[:, h * HD_ATT:(h + 1) * HD_ATT] = acc[h * n_new:(h + 1) * n_new,
                                                        h * HD_ATT:(h + 1) * HD_ATT].astype(o_ref.dtype)


def _fox_sample(layer, page_table, qbd, cache_kt, cache_vt, cache_ft, kt_new, vt_new, ft_new, n_new):
    bsz, n_pages = page_table.shape
    page = cache_kt.shape[3]
    np_ = PAGES_PER_STEP
    while n_pages % np_:
        np_ //= 2
    rows = qbd.shape[1]

    def page_spec(r, j):
        return pl.BlockSpec((None, None, r, page), lambda b, c, pt: (layer, pt[b, c * np_ + j], 0, 0))

    def seq_spec(r, w):
        return pl.BlockSpec((None, r, w), lambda b, c, pt: (b, 0, 0))

    in_specs = ([seq_spec(rows, W_ATT)]
                + [page_spec(W_ATT, j) for j in range(np_)] + [page_spec(W_ATT, j) for j in range(np_)]
                + [page_spec(H_ATT, j) for j in range(np_)]
                + [seq_spec(W_ATT, page), seq_spec(W_ATT, page), seq_spec(H_ATT, page)])
    grid_spec = pltpu.PrefetchScalarGridSpec(
        num_scalar_prefetch=1, grid=(bsz, n_pages // np_), in_specs=in_specs,
        out_specs=seq_spec(n_new, W_ATT),
        scratch_shapes=[pltpu.VMEM((rows, 1), F32), pltpu.VMEM((rows, 1), F32), pltpu.VMEM((rows, W_ATT), F32),
                        pltpu.VMEM((H_ATT, LANES), F32),
                        pltpu.VMEM((W_ATT, np_ * page), BF16), pltpu.VMEM((W_ATT, np_ * page), BF16)])
    return pl.pallas_call(
        functools.partial(_fox_sample_body, n_pages_step=np_, n_new=n_new),
        grid_spec=grid_spec, out_shape=jax.ShapeDtypeStruct((bsz, n_new, W_ATT), BF16),
        compiler_params=_cparams(("parallel", "arbitrary"), 56), name="fox_sample")(
            page_table, qbd, *([cache_kt] * np_), *([cache_vt] * np_), *([cache_ft] * np_), kt_new, vt_new, ft_new)


def _convb_body(*refs, tm, has_prev):
    if has_prev:
        u_ref, prev_ref, hist0_ref, w_ref, b_ref, g_ref, beta_ref, o_ref, ext_ref = refs
        hist = jnp.where(pl.program_id(1) == 0, hist0_ref[...], prev_ref[...])
    else:
        u_ref, hist0_ref, w_ref, b_ref, g_ref, beta_ref, o_ref, ext_ref = refs
        hist = hist0_ref[...]
    halo = CONVB_HALO
    ext_ref[0:halo, :] = hist
    ext_ref[halo:halo + tm, :] = u_ref[...]
    lead = halo - (CONV_WIDTH - 1)
    ch = min(CONV_CHUNK, tm)
    for r0 in range(0, tm, ch):
        acc = jnp.broadcast_to(b_ref[...], (ch, b_ref.shape[1]))
        for r in range(SUBLANES):
            taps = [k for k in range(CONV_WIDTH) if (k + lead) % SUBLANES == r]
            if not taps:
                continue
            win = ext_ref[r0 + r:r0 + taps[-1] + lead + ch, :]
            for k in taps:
                off = k + lead - r
                wk = jnp.concatenate([w_ref[k]] * (ch // SUBLANES), axis=0)
                acc = acc + wk * win[off:off + ch, :]
        y = _layer_norm(acc, g_ref[...], beta_ref[...])
        o_ref[r0:r0 + ch, :] = (y * jax.nn.sigmoid(y)).astype(o_ref.dtype)


def _convb(u, hist0, w, b, g, beta):
    bsz, t, c = u.shape
    tm = min(SEQ_TILE, t)
    nt = t // tm
    has_prev = nt > 1
    halo = CONVB_HALO
    r = tm // halo if has_prev else 1
    in_specs = [pl.BlockSpec((None, tm, c), lambda bi, i: (bi, i, 0))]
    args = [u]
    if has_prev:
        in_specs.append(pl.BlockSpec((None, halo, c), lambda bi, i: (bi, jnp.maximum(i * r - 1, 0), 0)))
        args.append(u)
    in_specs += [pl.BlockSpec((None, halo, c), lambda bi, i: (bi, 0, 0)),
                 _const_spec(w.shape), _const_spec(b.shape), _const_spec(g.shape), _const_spec(beta.shape)]
    args += [hist0, w, b, g, beta]
    return pl.pallas_call(
        functools.partial(_convb_body, tm=tm, has_prev=has_prev), grid=(bsz, nt),
        in_specs=in_specs, out_specs=pl.BlockSpec((None, tm, c), lambda bi, i: (bi, i, 0)),
        out_shape=jax.ShapeDtypeStruct((bsz, t, c), BF16),
        scratch_shapes=[pltpu.VMEM((halo + tm, c), F32)],
        compiler_params=_cparams(("parallel", "arbitrary"), 24), name="conformer_conv")(*args)


def _rglru_body(*refs, tm, has_prev):
    if has_prev:
        (x_ref, prev_ref, hist0_ref, gg_ref, h0_ref, w_ref, b_ref, wa_ref, ba_ref, wx_ref, bx_ref, lam_ref,
         o_ref, hl_ref, ext_ref, h_ref) = refs
        hist = jnp.where(pl.program_id(1) == 0, hist0_ref[...], prev_ref[...])
    else:
        (x_ref, hist0_ref, gg_ref, h0_ref, w_ref, b_ref, wa_ref, ba_ref, wx_ref, bx_ref, lam_ref,
         o_ref, hl_ref, ext_ref, h_ref) = refs
        hist = hist0_ref[...]

    @pl.when(pl.program_id(1) == 0)
    def _():
        h_ref[...] = jnp.broadcast_to(h0_ref[...], h_ref.shape)

    halo = CONVC_HALO
    ext_ref[0:halo, :] = hist
    ext_ref[halo:halo + tm, :] = x_ref[...]
    lead = halo - (RG_CONV_WIDTH - 1)
    xc = jnp.broadcast_to(b_ref[...], x_ref.shape)
    for k in range(RG_CONV_WIDTH):
        xc = xc + w_ref[k:k + 1, :] * ext_ref[k + lead:k + lead + tm, :]
    xcb = xc.astype(BF16)
    r = jax.nn.sigmoid(_dot(xcb, wa_ref[...]) + ba_ref[...])
    ig = jax.nn.sigmoid(_dot(xcb, wx_ref[...]) + bx_ref[...])
    log_a = -RG_C * r * _softplus(-lam_ref[...])
    a = jnp.exp(log_a)
    u = jnp.sqrt(-jnp.tanh(log_a) * (1.0 + a * a)) * (ig * xc)
    row = lax.broadcasted_iota(jnp.int32, a.shape, 0)
    d = 1
    while d < tm:
        a_sh = pltpu.roll(a, d, axis=0)
        u_sh = pltpu.roll(u, d, axis=0)
        keep = row >= d
        u = jnp.where(keep, a * u_sh + u, u)
        a = jnp.where(keep, a * a_sh, a)
        d *= 2
    h = a * h_ref[0:1, :] + u
    o_ref[...] = (h * gg_ref[...]).astype(o_ref.dtype)
    last = h[tm - 1:tm, :]
    h_ref[...] = jnp.broadcast_to(last, h_ref.shape)
    hl_ref[...] = last


def _rglru(x, gg, hist0, h0, w, b, wa, ba, wx, bx, lam):
    bsz, t, c = x.shape
    tm = min(SEQ_TILE, t)
    nt = t // tm
    has_prev = nt > 1
    halo = CONVC_HALO
    r = tm // halo if has_prev else 1
    tile = pl.BlockSpec((None, tm, c), lambda bi, i: (bi, i, 0))
    in_specs = [tile]
    args = [x]
    if has_prev:
        in_specs.append(pl.BlockSpec((None, halo, c), lambda bi, i: (bi, jnp.maximum(i * r - 1, 0), 0)))
        args.append(x)
    in_specs += [pl.BlockSpec((None, halo, c), lambda bi, i: (bi, 0, 0)), tile,
                 pl.BlockSpec((None, 1, c), lambda bi, i: (bi, 0, 0))]
    args += [hist0, gg, h0]
    consts = [w, b, wa, ba, wx, bx, lam]
    in_specs += [_const_spec(a.shape) for a in consts]
    args += consts
    return pl.pallas_call(
        functools.partial(_rglru_body, tm=tm, has_prev=has_prev), grid=(bsz, nt),
        in_specs=in_specs,
        out_specs=(tile, pl.BlockSpec((None, 1, c), lambda bi, i: (bi, 0, 0))),
        out_shape=(jax.ShapeDtypeStruct((bsz, t, c), BF16), jax.ShapeDtypeStruct((bsz, 1, c), F32)),
        scratch_shapes=[pltpu.VMEM((halo + tm, c), F32), pltpu.VMEM((SUBLANES, c), F32)],
        compiler_params=_cparams(("parallel", "arbitrary"), 24), name="rglru")(*args)


def _merge_body(x_ref, oa_ref, ub_ref, hc_ref, gates_ref, wa_ref, wb_ref, bb_ref, wc_ref, wo_ref, g_ref, b_ref,
                wr_ref, wrl_ref, br_ref, x1_ref, x1p_ref, idx_ref, tw_ref, *, alpha):
    d = x_ref.shape[1]
    ya = _dot(oa_ref[...], wa_ref[...])
    yb = _dot(ub_ref[...], wb_ref[...]) + bb_ref[...]
    yc = _dot(hc_ref[...], wc_ref[...])
    merged = gates_ref[:, 0:d] * ya + gates_ref[:, d:2 * d] * yb + gates_ref[:, 2 * d:3 * d] * yc
    x1 = _layer_norm(alpha * x_ref[...] + _dot(merged.astype(BF16), wo_ref[...]), g_ref[...], b_ref[...])
    x1_ref[...] = x1
    words = _pack_halves(x1)
    for p in range(x1p_ref.shape[0]):
        x1p_ref[p] = words[:, p * PLANE_WIDTH:(p + 1) * PLANE_WIDTH]
    x_hi = x1.astype(BF16)
    x_lo = (x1 - x_hi.astype(F32)).astype(BF16)
    logits = (_dot(x_hi, wr_ref[...]) + (_dot(x_lo, wr_ref[...]) + _dot(x_hi, wrl_ref[...]))) + br_ref[...]
    lane = lax.broadcasted_iota(jnp.int32, logits.shape, 1)
    vals = logits
    idx_out = jnp.zeros(logits.shape, jnp.int32)
    val_out = jnp.full(logits.shape, NEG_BIG, F32)
    for k in range(TOP_K):
        m = jnp.max(vals, axis=-1, keepdims=True)
        idx = jnp.min(jnp.where(vals == m, lane, LANES), axis=-1, keepdims=True)
        idx_out = jnp.where(lane == k, idx, idx_out)
        val_out = jnp.where(lane == k, m, val_out)
        vals = jnp.where(lane == idx, -jnp.inf, vals)
    e = jnp.exp(val_out - jnp.max(val_out, axis=-1, keepdims=True))
    e = jnp.where(lane < TOP_K, e, 0.0)
    idx_ref[...] = idx_out[:, :TOP_K]
    tw_ref[...] = (e / jnp.sum(e, axis=-1, keepdims=True))[:, :TOP_K]


def _merge(x, oa, ub, hc, gates, wts, alpha):
    n, d = x.shape
    tm = min(MERGE_TILE, n)

    def row(c):
        return pl.BlockSpec((tm, c), lambda i: (i, 0))

    return pl.pallas_call(
        functools.partial(_merge_body, alpha=alpha), grid=(n // tm,),
        in_specs=[row(d), row(oa.shape[1]), row(ub.shape[1]), row(hc.shape[1]), row(gates.shape[1])]
        + [_const_spec(a.shape) for a in wts],
        out_specs=(row(d), pl.BlockSpec((d // 2 // PLANE_WIDTH, tm, PLANE_WIDTH), lambda i: (0, i, 0)),
                   row(TOP_K), row(TOP_K)),
        out_shape=(jax.ShapeDtypeStruct((n, d), F32),
                   jax.ShapeDtypeStruct((d // 2 // PLANE_WIDTH, n, PLANE_WIDTH), jnp.int32),
                   jax.ShapeDtypeStruct((n, TOP_K), jnp.int32), jax.ShapeDtypeStruct((n, TOP_K), F32)),
        compiler_params=_cparams(("parallel",), 40), name="merge_router")(x, oa, ub, hc, gates, *wts)


def _moe_body(te_ref, gs_ref, tr_ref, nx_ref, sl_ref, xs_ref, wgu_hbm, bgu_ref, wd_hbm, bd_ref, ys_ref,
              wgu_buf, wd_buf, wgub_ref, wdb_ref, sem, *, layer):
    i = pl.program_id(0)
    n_planes, tm, pw = xs_ref.shape

    def weight_copies(expert, slot):
        return (pltpu.make_async_copy(wgu_hbm.at[layer, expert], wgu_buf.at[slot], sem.at[0, slot]),
                pltpu.make_async_copy(wd_hbm.at[layer, expert], wd_buf.at[slot], sem.at[1, slot]))

    @pl.when(i == 0)
    def _():
        for c in weight_copies(te_ref[0], 0):
            c.start()

    @pl.when(gs_ref[i] == 1)
    def _():
        slot = sl_ref[i]
        for c in weight_copies(te_ref[i], slot):
            c.wait()

        @pl.when(nx_ref[i] >= 0)
        def _():
            for c in weight_copies(nx_ref[i], 1 - slot):
                c.start()

        wgub_ref[...] = wgu_buf[slot].astype(BF16)
        wdb_ref[...] = wd_buf[slot].astype(BF16)

    @pl.when(tr_ref[i] > 0)
    def _():
        dff = wdb_ref.shape[0]
        live = lax.broadcasted_iota(jnp.int32, (tm, pw), 0) < tr_ref[i]
        half = n_planes * pw
        gu = bgu_ref[...]
        for p in range(n_planes):
            lo, hi = _unpack_halves(jnp.where(live, xs_ref[p], 0))
            gu = gu + _dot(lo.astype(BF16), wgub_ref[p * pw:(p + 1) * pw, :])
            gu = gu + _dot(hi.astype(BF16), wgub_ref[half + p * pw:half + (p + 1) * pw, :])
        gate = jnp.minimum(gu[:, :dff], SWIGLU_LIMIT)
        up = jnp.clip(gu[:, dff:], -SWIGLU_LIMIT, SWIGLU_LIMIT)
        hid = (up + 1.0) * gate * jax.nn.sigmoid(SWIGLU_ALPHA * gate)
        words = _pack_halves(_dot(hid.astype(BF16), wdb_ref[...]) + bd_ref[...])
        for p in range(n_planes):
            ys_ref[p] = words[:, p * pw:(p + 1) * pw]

    @pl.when(tr_ref[i] == 0)
    def _():
        ys_ref[...] = jnp.zeros_like(ys_ref)


def _moe_experts(layer, tables, xs, w_gate_up, b_gate_up, w_down, b_down):
    n_planes, r, pw = xs.shape
    d = 2 * n_planes * pw
    tm = MOE_TILE
    n_exp, _, dgu = w_gate_up.shape[1:]
    dff = w_down.shape[2]
    bgu = b_gate_up.reshape(-1, 1, dgu)
    bd = b_down.reshape(-1, 1, d)
    grid_spec = pltpu.PrefetchScalarGridSpec(
        num_scalar_prefetch=len(tables), grid=(r // tm,),
        in_specs=[pl.BlockSpec((n_planes, tm, pw), lambda i, te, *_: (0, i, 0)),
                  pl.BlockSpec(memory_space=pl.ANY),
                  pl.BlockSpec((None, 1, dgu), lambda i, te, *_: (layer * n_exp + te[i], 0, 0)),
                  pl.BlockSpec(memory_space=pl.ANY),
                  pl.BlockSpec((None, 1, d), lambda i, te, *_: (layer * n_exp + te[i], 0, 0))],
        out_specs=pl.BlockSpec((n_planes, tm, pw), lambda i, te, *_: (0, i, 0)),
        scratch_shapes=[pltpu.VMEM((2, d, dgu), F32), pltpu.VMEM((2, dff, d), F32),
                        pltpu.VMEM((d, dgu), BF16), pltpu.VMEM((dff, d), BF16),
                        pltpu.SemaphoreType.DMA((2, 2))])
    return pl.pallas_call(
        functools.partial(_moe_body, layer=layer), grid_spec=grid_spec,
        out_shape=jax.ShapeDtypeStruct((n_planes, r, pw), jnp.int32),
        compiler_params=_cparams(("arbitrary",), 56), name="moe_experts")(
            *tables, xs, w_gate_up, bgu, w_down, bd)


def _combine_body(x1_ref, yg_ref, tw_ref, g_ref, b_ref, o_ref, *, alpha):
    tw = tw_ref[...]
    n_planes = yg_ref.shape[1]
    lows, highs = [], []
    for p in range(n_planes):
        lo_acc = hi_acc = None
        for k in range(TOP_K):
            lo, hi = _unpack_halves(yg_ref[k, p])
            wk = tw[:, k:k + 1]
            lo_acc = wk * lo if lo_acc is None else lo_acc + wk * lo
            hi_acc = wk * hi if hi_acc is None else hi_acc + wk * hi
        lows.append(lo_acc)
        highs.append(hi_acc)
    y = jnp.concatenate(lows + highs, axis=1)
    o_ref[...] = _layer_norm(alpha * x1_ref[...] + y, g_ref[...], b_ref[...])


def _combine(x1, yg, tw, g, b, alpha):
    n, d = x1.shape
    n_planes, pw = yg.shape[1], yg.shape[3]
    tm = min(MERGE_TILE, n)
    return pl.pallas_call(
        functools.partial(_combine_body, alpha=alpha), grid=(n // tm,),
        in_specs=[pl.BlockSpec((tm, d), lambda i: (i, 0)),
                  pl.BlockSpec((TOP_K, n_planes, tm, pw), lambda i: (0, 0, i, 0)),
                  pl.BlockSpec((tm, TOP_K), lambda i: (i, 0)), _const_spec(g.shape), _const_spec(b.shape)],
        out_specs=pl.BlockSpec((tm, d), lambda i: (i, 0)), out_shape=jax.ShapeDtypeStruct((n, d), F32),
        compiler_params=_cparams(("parallel",), 32), name="moe_combine_norm")(x1, yg, tw, g, b)


def _dispatch_body(idx_ref, pos_ref, meta_ref, cnt_ref, base_ref):
    ph = pl.program_id(0)
    i = pl.program_id(1)
    tb = idx_ref.shape[0]
    lane = lax.broadcasted_iota(jnp.int32, (tb, LANES), 1)
    idx = idx_ref[...]
    onehot = [(idx[:, k:k + 1] == lane).astype(F32) for k in range(TOP_K)]
    colsum = [jnp.sum(o, axis=0, keepdims=True) for o in onehot]

    @pl.when(jnp.logical_and(ph == 0, i == 0))
    def _():
        cnt_ref[...] = jnp.zeros_like(cnt_ref)

    @pl.when(ph == 0)
    def _():
        cnt_ref[...] = cnt_ref[...] + (colsum[0] + colsum[1] + colsum[2] + colsum[3])

    @pl.when(jnp.logical_and(ph == 1, i == 0))
    def _():
        cnt = cnt_ref[...]
        ptiles = jnp.floor((cnt + (MOE_TILE - 1)) * (1.0 / MOE_TILE))
        tile_end = _lane_cumsum(ptiles)
        base_ref[...] = (tile_end - ptiles) * MOE_TILE
        row = lax.broadcasted_iota(jnp.int32, meta_ref.shape, 0)
        meta_ref[...] = jnp.where(row == 0, tile_end[0:1, :], jnp.where(row == 1, cnt[0:1, :], 0.0)).astype(jnp.int32)

    @pl.when(ph == 1)
    def _():
        earlier = (lax.broadcasted_iota(jnp.int32, (tb, tb), 0) > lax.broadcasted_iota(jnp.int32, (tb, tb), 1))
        earlier = earlier.astype(BF16)
        base = base_ref[0:1, :]
        slot = lax.broadcasted_iota(jnp.int32, (tb, TOP_K), 1)
        pos = jnp.zeros((tb, TOP_K), F32)
        for k in range(TOP_K):
            seen = _dot(earlier, onehot[k].astype(BF16)) + base
            pos = jnp.where(slot == k, jnp.sum(onehot[k] * seen, axis=1, keepdims=True), pos)
            base = base + colsum[k]
        base_ref[...] = jnp.broadcast_to(base, base_ref.shape)
        pos_ref[...] = pos.astype(jnp.int32)


def _dispatch(top_idx):
    n = top_idx.shape[0]
    tm = MOE_TILE
    n_tiles = -(-(n * TOP_K) // tm) + N_EXPERTS
    tb = next(t for t in DISPATCH_TILES if n % t == 0)
    pos, meta = pl.pallas_call(
        _dispatch_body, grid=(2, n // tb),
        in_specs=[pl.BlockSpec((tb, TOP_K), lambda ph, i: (i, 0))],
        out_specs=(pl.BlockSpec((tb, TOP_K), lambda ph, i: (i * ph, 0)), _const_spec((SUBLANES, LANES))),
        out_shape=(jax.ShapeDtypeStruct((n, TOP_K), jnp.int32), jax.ShapeDtypeStruct((SUBLANES, LANES), jnp.int32)),
        scratch_shapes=[pltpu.VMEM((SUBLANES, LANES), F32), pltpu.VMEM((SUBLANES, LANES), F32)],
        compiler_params=_cparams(("arbitrary", "arbitrary"), 32), name="moe_dispatch")(top_idx)
    tile_end, cnt = meta[0, :N_EXPERTS], meta[1, :N_EXPERTS]
    experts = jnp.arange(N_EXPERTS, dtype=jnp.int32)
    tiles = jnp.arange(n_tiles, dtype=jnp.int32)
    te = jnp.minimum(jnp.sum((tiles[:, None] >= tile_end[None, :]).astype(jnp.int32), axis=1), N_EXPERTS - 1)
    of_tile = (te[:, None] == experts[None, :]).astype(jnp.int32)

    def per_tile(v):
        return jnp.sum(of_tile * v[None, :], axis=1)

    tile_start = tile_end - (cnt + tm - 1) // tm
    tr = jnp.clip(per_tile(cnt) - (tiles - per_tile(tile_start)) * tm, 0, tm)
    tr = tr * (tiles < tile_end[-1]).astype(jnp.int32)
    gs = jnp.logical_and(tiles == per_tile(tile_start), tr > 0).astype(jnp.int32)
    later = jnp.logical_and(experts[None, :] > experts[:, None], cnt[None, :] > 0)
    nxt = jnp.min(jnp.where(later, experts[None, :], N_EXPERTS), axis=1)
    nxt = jnp.where(nxt == N_EXPERTS, -1, nxt)
    order = jnp.sum(jnp.logical_and(experts[None, :] < experts[:, None], cnt[None, :] > 0).astype(jnp.int32), axis=1)
    return pos, (te, gs, tr, per_tile(nxt), per_tile(order % 2)), n_tiles * tm


def _sc_mesh():
    return plsc.VectorSubcoreMesh(core_axis_name="c", subcore_axis_name="s")


def _sc_gather_rows(table, idx):
    m = idx.shape[0]
    w = table.shape[1]

    @functools.partial(pl.kernel, out_type=jax.ShapeDtypeStruct((m, w), table.dtype), mesh=_sc_mesh())
    def gather(t_hbm, i_hbm, o_hbm):
        def body(i_vmem, o_vmem):
            pltpu.sync_copy(t_hbm.at[i_vmem.at[0]], o_vmem)

        pltpu.emit_pipeline(
            body, grid=(m // SC_WINDOW,),
            in_specs=[pl.BlockSpec((1, SC_WINDOW), index_map=lambda i: (0, i))],
            out_specs=[pl.BlockSpec((SC_WINDOW, w), index_map=lambda i: (i, 0))],
            core_axis_name=("c", "s"), dimension_semantics=(pltpu.PARALLEL,))(i_hbm, o_hbm)

    return gather(table, idx.reshape(1, m))


def _sc_scatter_rows(rows, idx, n_out):
    n_src, w = rows.shape
    m = idx.shape[0]
    src_blocks = n_src // SC_WINDOW

    @functools.partial(pl.kernel, out_type=jax.ShapeDtypeStruct((n_out, w), rows.dtype), mesh=_sc_mesh(),
                       scratch_types=[])
    def scatter(x_hbm, i_hbm, o_hbm):
        def body(x_vmem, i_vmem):
            pltpu.sync_copy(x_vmem, o_hbm.at[i_vmem.at[0]])

        pltpu.emit_pipeline(
            body, grid=(m // SC_WINDOW,),
            in_specs=[pl.BlockSpec((SC_WINDOW, w), index_map=lambda i: (lax.rem(i, src_blocks), 0)),
                      pl.BlockSpec((1, SC_WINDOW), index_map=lambda i: (0, i))],
            out_specs=[], core_axis_name=("c", "s"), dimension_semantics=(pltpu.PARALLEL,))(x_hbm, i_hbm)

    return scatter(rows, idx.reshape(1, m))


def _plane_rows(pos, n_planes, rows_per_plane):
    off = jnp.arange(n_planes, dtype=jnp.int32) * rows_per_plane
    return (pos.T[:, None, :] + off[None, :, None]).reshape(-1)


def _pad_rows(a, rows):
    return jnp.pad(a, ((0, 0), (rows - a.shape[1], 0), (0, 0)))


def _block_diag(w):
    h, bd, _ = w.shape
    eye = jnp.eye(h, dtype=w.dtype)
    return jnp.einsum("hij,hg->higj", w, eye).reshape(h * bd, h * bd)


def kernel(x_prompt, x_sample, cache_k, cache_v, cache_logf, state_conv_b, state_conv_c, state_h, page_table,
           w_in, b_f, w_att_o, conv_b_w, conv_b_b, lnb_g, lnb_b, w_conv_o, b_conv_o, conv_c_w, conv_c_b,
           w_rg_a, b_rg_a, w_rg_x, b_rg_x, rg_lambda, w_rg_o, w_out, ln1_g, ln1_b, w_router, b_router,
           w_gate_up, b_gate_up, w_down, b_down, ln2_g, ln2_b):
    bp, seq, d = x_prompt.shape
    bs, ns, _ = x_sample.shape
    depth, n_pool, page = cache_k.shape[:3]
    w_conv = state_conv_b.shape[-1]
    w_rg = state_conv_c.shape[-1]
    alpha = (2 * depth) ** 0.25
    n_p, n_s = bp * seq, bs * ns
    off_f = 3 * W_ATT
    off_glu = off_f + H_ATT
    off_rg = off_glu + 2 * w_conv
    off_gate = off_rg + 2 * w_rg

    ckt = cache_k.transpose(0, 1, 3, 4, 2).reshape(depth, n_pool, W_ATT, page)
    cvt = cache_v.transpose(0, 1, 3, 4, 2).reshape(depth, n_pool, W_ATT, page)
    cft = cache_logf.transpose(0, 1, 3, 2)
    head_eye = jnp.eye(H_ATT, dtype=BF16)

    xp = x_prompt.reshape(n_p, d)
    xs = x_sample.reshape(n_s, d)
    outs = {k: [] for k in ("ks", "vs", "fs", "cbp", "cbs", "ccp", "ccs", "hp", "hs")}
    stacked = None

    for l in range(depth):
        wl = w_in[l]
        wf = wl[:, off_f:off_glu]
        in_w = (wl[:, :2 * W_ATT].astype(BF16), wl[:, :off_f].T.astype(BF16),
                jnp.pad(wf, ((0, 0), (0, LANES - H_ATT))).astype(BF16), wf.T.astype(BF16),
                b_f[l].reshape(1, H_ATT), b_f[l].reshape(H_ATT, 1),
                wl[:, off_glu:off_rg].astype(BF16), wl[:, off_rg:off_gate].astype(BF16),
                wl[:, off_gate:].astype(BF16))
        cbw = jnp.broadcast_to(conv_b_w[l][:, None, :], (CONV_WIDTH, SUBLANES, w_conv))
        ccw = jnp.pad(conv_c_w[l], ((0, CONVC_HALO - RG_CONV_WIDTH), (0, 0)))
        rg_w = (ccw, conv_c_b[l].reshape(1, -1), _block_diag(w_rg_a[l]).astype(BF16), b_rg_a[l].reshape(1, -1),
                _block_diag(w_rg_x[l]).astype(BF16), b_rg_x[l].reshape(1, -1), rg_lambda[l].reshape(1, -1))
        cb_w = (cbw, conv_b_b[l].reshape(1, -1), lnb_g[l].reshape(1, -1), lnb_b[l].reshape(1, -1))
        wr = jnp.pad(w_router[l], ((0, 0), (0, LANES - N_EXPERTS)))
        br = jnp.pad(b_router[l], (0, LANES - N_EXPERTS), constant_values=NEG_BIG).reshape(1, LANES)
        mg_w = (w_att_o[l].astype(BF16), w_conv_o[l].astype(BF16), b_conv_o[l].reshape(1, -1),
                w_rg_o[l].astype(BF16), w_out[l].astype(BF16), ln1_g[l].reshape(1, -1), ln1_b[l].reshape(1, -1),
                wr.astype(BF16), (wr - wr.astype(BF16).astype(F32)).astype(BF16), br)

        (q, kb, qtm, kt, vt, vtb, lf, lft, u, rgx, gg, gates) = _inproj(xp, in_w, seq, l, depth, stacked)
        stacked = (kt, vt, lft)
        oa = _fox_prompt(qtm, kb, vtb, _cumsum(lf, seq), bp, seq)
        u3 = u.reshape(bp, seq, w_conv)
        ub = _convb(u3, jnp.zeros((bp, CONVB_HALO, w_conv), F32), *cb_w)
        rgx3 = rgx.reshape(bp, seq, w_rg)
        hc, hl = _rglru(rgx3, gg.reshape(bp, seq, w_rg), jnp.zeros((bp, CONVC_HALO, w_rg), F32),
                        jnp.zeros((bp, 1, w_rg), F32), *rg_w)
        x1_p, x1p, tip, twp = _merge(xp, oa, ub.reshape(n_p, w_conv), hc.reshape(n_p, w_rg), gates, mg_w, alpha)
        outs["cbp"].append(u3[:, seq - (CONV_WIDTH - 1):])
        outs["ccp"].append(rgx3[:, seq - (RG_CONV_WIDTH - 1):])
        outs["hp"].append(hl.reshape(bp, w_rg))

        (q, kb, qtm, kt, vt, vtb, lf, lft, u, rgx, gg, gates) = _inproj(xs, in_w, n_s)
        qbd = jnp.einsum("bthd,hg->bhtgd", q.reshape(bs, ns, H_ATT, HD_ATT), head_eye)
        qbd = qbd.reshape(bs, H_ATT * ns, W_ATT)

        def new_page(a):
            a = a.reshape(a.shape[0], bs, ns).transpose(1, 0, 2)
            return jnp.pad(a, ((0, 0), (0, 0), (0, page - ns)))

        kt, vt, lft = kt[0, 0], vt[0, 0], lft[0, 0]
        oa = _fox_sample(l, page_table, qbd, ckt, cvt, cft, new_page(kt), new_page(vt), new_page(lft), ns)
        u3 = u.reshape(bs, ns, w_conv)
        ub = _convb(u3, _pad_rows(state_conv_b[l], CONVB_HALO), *cb_w)
        rgx3 = rgx.reshape(bs, ns, w_rg)
        hc, hl = _rglru(rgx3, gg.reshape(bs, ns, w_rg), _pad_rows(state_conv_c[l], CONVC_HALO),
                        state_h[l].reshape(bs, 1, w_rg), *rg_w)
        x1_s, x1s, tis, tws = _merge(xs, oa.reshape(n_s, W_ATT), ub.reshape(n_s, w_conv), hc.reshape(n_s, w_rg),
                                     gates, mg_w, alpha)
        outs["ks"].append(kt.T.reshape(bs, ns, H_ATT, HD_ATT))
        outs["vs"].append(vt.T.reshape(bs, ns, H_ATT, HD_ATT))
        outs["fs"].append(lf.reshape(bs, ns, H_ATT))
        outs["cbs"].append(jnp.concatenate([state_conv_b[l], u3], axis=1)[:, ns:])
        outs["ccs"].append(jnp.concatenate([state_conv_c[l], rgx3], axis=1)[:, ns:])
        outs["hs"].append(hl.reshape(bs, w_rg))

        pos, tables, n_rows = _dispatch(jnp.concatenate([tip, tis], axis=0))
        x1_all = jnp.concatenate([x1p, x1s], axis=1)
        n_planes, n_all, pw = x1_all.shape
        xs_rows = _sc_scatter_rows(x1_all.reshape(n_planes * n_all, pw), _plane_rows(pos, n_planes, n_rows),
                                   n_planes * n_rows)
        ys = _moe_experts(l, tables, xs_rows.reshape(n_planes, n_rows, pw), w_gate_up, b_gate_up, w_down, b_down)
        ys = ys.reshape(n_planes * n_rows, pw)
        g2, b2 = ln2_g[l].reshape(1, -1), ln2_b[l].reshape(1, -1)
        yg = _sc_gather_rows(ys, _plane_rows(pos[:n_p], n_planes, n_rows)).reshape(TOP_K, n_planes, n_p, pw)
        xp = _combine(x1_p, yg, twp, g2, b2, alpha)
        yg = _sc_gather_rows(ys, _plane_rows(pos[n_p:], n_planes, n_rows)).reshape(TOP_K, n_planes, n_s, pw)
        xs = _combine(x1_s, yg, tws, g2, b2, alpha)

    st = {k: jnp.stack(v, 0) for k, v in outs.items()}
    kt, vt, lft = stacked
    st["kp"] = kt.reshape(depth, bp, H_ATT, HD_ATT, seq).transpose(0, 1, 4, 2, 3)
    st["vp"] = vt.reshape(depth, bp, H_ATT, HD_ATT, seq).transpose(0, 1, 4, 2, 3)
    st["fp"] = lft.transpose(0, 1, 3, 2)
    return (xp.reshape(bp, seq, d), xs.reshape(bs, ns, d), st["kp"], st["vp"], st["fp"], st["ks"], st["vs"],
            st["fs"], st["cbp"], st["cbs"], st["ccp"], st["ccs"], st["hp"], st["hs"])
```
